```python
import math
import jax, jax.numpy as jnp
from jax import lax
import numpy as np

D_MODEL = 2048
BATCH = 1
SEQ = 8192
DEPTH = 4

ROPE_THETA = 10000.0
NORM_EPS = 1e-6
RET_HEADS = 8
RET_DK = D_MODEL // 32
RET_DV = D_MODEL // 16
RET_CHUNK = 128
CONV_WIDTH = D_MODEL // 2
CONV_K = 3
DIFF_HEADS = 8
DIFF_HEAD_DIM = D_MODEL // (2 * DIFF_HEADS)
ATTN_BLOCK = 128
FFN_HIDDEN = -(-8 * D_MODEL // (3 * 256)) * 256
RET_QK = RET_HEADS * RET_DK
RET_V = RET_HEADS * RET_DV
HYB_IN = 2 * RET_QK + 2 * RET_V + 3 * CONV_WIDTH
HYB_CAT = RET_V + CONV_WIDTH
DIFF_QKV = 2 * DIFF_HEADS * 2 * DIFF_HEAD_DIM + DIFF_HEADS * 2 * DIFF_HEAD_DIM
DIFF_CAT = DIFF_HEADS * 2 * DIFF_HEAD_DIM
N_EVEN = (DEPTH + 1) // 2
N_ODD = DEPTH // 2

kernel_name = "hybrid_retention_shortconv_diffattn_swiglu"


def _rms(x):
    xf = x.astype(jnp.float32)
    return xf * lax.rsqrt(jnp.mean(xf * xf, axis=-1, keepdims=True) + NORM_EPS)


def rms_norm(x, g):
    return (_rms(x) * g.astype(jnp.float32)).astype(x.dtype)


def rope_tables(seq, dim):
    inv = ROPE_THETA ** (-jnp.arange(0, dim, 2, dtype=jnp.float32) / dim)
    ang = jnp.arange(seq, dtype=jnp.float32)[:, None] * inv[None, :]
    return jnp.cos(ang), jnp.sin(ang)


def apply_rope(x, cos, sin):
    half = x.shape[-1] // 2
    shape = (1, cos.shape[0]) + (1,) * (x.ndim - 3) + (half,)
    c = cos.reshape(shape).astype(x.dtype)
    s = sin.reshape(shape).astype(x.dtype)
    x1, x2 = x[..., :half], x[..., half:]
    return jnp.concatenate([x1 * c - x2 * s, x1 * s + x2 * c], axis=-1)


def retention_chunkwise(q, k, v):
    B, S, H, dk = q.shape
    dv = v.shape[-1]
    C = RET_CHUNK
    n = S // C
    gamma = 1.0 - jnp.exp2(-5.0 - jnp.arange(H, dtype=jnp.float32))
    log_g = jnp.log(gamma)
    idx = jnp.arange(C, dtype=jnp.float32)
    dist = idx[:, None] - idx[None, :]
    decay_in = jnp.where(dist >= 0, jnp.exp(log_g[:, None, None] * jnp.maximum(dist, 0.0)), 0.0)
    q_dec = jnp.exp(log_g[:, None] * (idx + 1.0))[:, :, None]
    k_dec = jnp.exp(log_g[:, None] * (C - 1.0 - idx))[:, :, None]
    chunk_dec = jnp.exp(log_g * C)[:, None, None]

    def to_chunks(t):
        return t.reshape(B, n, C, H, t.shape[-1]).transpose(1, 0, 3, 2, 4)

    def step(state, inp):
        qi, ki, vi = inp
        inner = jnp.einsum('bhid,bhjd->bhij', qi, ki) * decay_in
        o = (jnp.einsum('bhij,bhje->bhie', inner, vi)
             + jnp.einsum('bhid,bhde->bhie', qi * q_dec, state))
        state = chunk_dec * state + jnp.einsum('bhjd,bhje->bhde', ki * k_dec, vi)
        return state, o

    state0 = jnp.zeros((B, H, dk, dv), jnp.float32)
    _, o = lax.scan(step, state0, (to_chunks(q), to_chunks(k), to_chunks(v)))
    return o.transpose(1, 0, 3, 2, 4).reshape(B, S, H, dv)


def causal_depthwise_conv(u, w):
    return lax.conv_general_dilated(
        u, w[:, None, :].astype(u.dtype), window_strides=(1,), padding=[(CONV_K - 1, 0)],
        dimension_numbers=('NWC', 'WIO', 'NWC'), feature_group_count=u.shape[-1])


def hybrid_retention_conv(h, w_in, conv_w, w_out, cos, sin):
    B, S, _ = h.shape
    proj = h @ w_in
    cuts = np.cumsum([RET_QK, RET_QK, RET_V, RET_V, CONV_WIDTH, CONV_WIDTH]).tolist()
    q, k, v, g, cb, cc, cx = jnp.split(proj, cuts, axis=-1)
    q = apply_rope(q.reshape(B, S, RET_HEADS, RET_DK), cos, sin)
    k = apply_rope(k.reshape(B, S, RET_HEADS, RET_DK), cos, sin) * (RET_DK ** -0.5)
    ret = retention_chunkwise(q.astype(jnp.float32), k.astype(jnp.float32),
                              v.reshape(B, S, RET_HEADS, RET_DV).astype(jnp.float32))
    ret = _rms(ret).reshape(B, S, RET_V)
    ret = (jax.nn.silu(g.astype(jnp.float32)) * ret).astype(h.dtype)
    conv_out = cb * causal_depthwise_conv(cc * cx, conv_w)
    return jnp.concatenate([ret, conv_out], axis=-1) @ w_out


def diff_attention(h, w_qkv, q_norm, k_norm, lq1, lk1, lq2, lk2, subln, w_out,
                   lambda_init, cos, sin):
    B, S, _ = h.shape
    H, d = DIFF_HEADS, DIFF_HEAD_DIM
    q, k, v = jnp.split(h @ w_qkv, [2 * H * d, 4 * H * d], axis=-1)
    q = apply_rope(rms_norm(q.reshape(B, S, H, 2, d), q_norm), cos, sin)
    k = apply_rope(rms_norm(k.reshape(B, S, H, 2, d), k_norm), cos, sin)
    v = v.reshape(B, S, H, 2 * d)
    f32 = jnp.float32
    lam = (jnp.exp(jnp.sum(lq1.astype(f32) * lk1.astype(f32)))
           - jnp.exp(jnp.sum(lq2.astype(f32) * lk2.astype(f32))) + lambda_init)
    scale = d ** -0.5
    kpos = jnp.arange(S)

    def block(i):
        start = i * ATTN_BLOCK
        qb = lax.dynamic_slice_in_dim(q, start, ATTN_BLOCK, axis=1)
        s = jnp.einsum('bqhmd,bkhmd->bhmqk', qb, k, preferred_element_type=f32) * scale
        qpos = start + jnp.arange(ATTN_BLOCK)
        s = jnp.where(kpos[None, :] <= qpos[:, None], s, -jnp.inf)
        p = jax.nn.softmax(s, axis=-1)
        w = p[:, :, 0] - lam * p[:, :, 1]
        return jnp.einsum('bhqk,bkhe->bqhe', w.astype(v.dtype), v)

    o = lax.map(block, jnp.arange(S // ATTN_BLOCK))
    o = o.transpose(1, 0, 2, 3, 4).reshape(B, S, H, 2 * d)
    o = rms_norm(o, subln) * (1.0 - lambda_init)
    return o.reshape(B, S, DIFF_CAT) @ w_out


def swiglu(h, w_gate, w_up, w_down):
    return (jax.nn.silu(h @ w_gate) * (h @ w_up)) @ w_down


def setup_inputs(seed: int = 0) -> dict:
    key = jax.random.key(seed)
    ks = jax.random.split(key, 18)
    nrm = jax.random.normal
    f32 = jnp.float32

    def gain(k, shape):
        return 1.0 + 0.02 * nrm(k, shape, f32)

    return {
        "x": nrm(ks[0], (BATCH, SEQ, D_MODEL), f32),
        "norm_mix": gain(ks[1], (DEPTH, D_MODEL)),
        "norm_ffn": gain(ks[2], (DEPTH, D_MODEL)),
        "hyb_w_in": nrm(ks[3], (N_EVEN, D_MODEL, HYB_IN), f32) * D_MODEL ** -0.5,
        "hyb_conv_w": nrm(ks[4], (N_EVEN, CONV_K, CONV_WIDTH), f32) * CONV_K ** -0.5,
        "hyb_w_out": nrm(ks[5], (N_EVEN, HYB_CAT, D_MODEL), f32) * HYB_CAT ** -0.5,
        "diff_w_qkv": nrm(ks[6], (N_ODD, D_MODEL, DIFF_QKV), f32) * D_MODEL ** -0.5,
        "diff_q_norm": gain(ks[7], (N_ODD, DIFF_HEAD_DIM)),
        "diff_k_norm": gain(ks[8], (N_ODD, DIFF_HEAD_DIM)),
        "diff_lambda_q1": 0.1 * nrm(ks[9], (N_ODD, DIFF_HEAD_DIM), f32),
        "diff_lambda_k1": 0.1 * nrm(ks[10], (N_ODD, DIFF_HEAD_DIM), f32),
        "diff_lambda_q2": 0.1 * nrm(ks[11], (N_ODD, DIFF_HEAD_DIM), f32),
        "diff_lambda_k2": 0.1 * nrm(ks[12], (N_ODD, DIFF_HEAD_DIM), f32),
        "diff_subln": gain(ks[13], (N_ODD, 2 * DIFF_HEAD_DIM)),
        "diff_w_out": nrm(ks[14], (N_ODD, DIFF_CAT, D_MODEL), f32) * DIFF_CAT ** -0.5,
        "ffn_w_gate": nrm(ks[15], (DEPTH, D_MODEL, FFN_HIDDEN), f32) * D_MODEL ** -0.5,
        "ffn_w_up": nrm(ks[16], (DEPTH, D_MODEL, FFN_HIDDEN), f32) * D_MODEL ** -0.5,
        "ffn_w_down": nrm(ks[17], (DEPTH, FFN_HIDDEN, D_MODEL), f32) * FFN_HIDDEN ** -0.5,
    }


def reference(x, norm_mix, norm_ffn, hyb_w_in, hyb_conv_w, hyb_w_out,
              diff_w_qkv, diff_q_norm, diff_k_norm, diff_lambda_q1, diff_lambda_k1,
              diff_lambda_q2, diff_lambda_k2, diff_subln, diff_w_out,
              ffn_w_gate, ffn_w_up, ffn_w_down):
    S = x.shape[1]
    cos_r, sin_r = rope_tables(S, RET_DK)
    cos_a, sin_a = rope_tables(S, DIFF_HEAD_DIM)
    for layer in range(DEPTH):
        h = rms_norm(x, norm_mix[layer])
        j = layer // 2
        if layer % 2 == 0:
            x = x + hybrid_retention_conv(h, hyb_w_in[j], hyb_conv_w[j], hyb_w_out[j],
                                          cos_r, sin_r)
        else:
            lambda_init = 0.8 - 0.6 * math.exp(-0.3 * layer)
            x = x + diff_attention(h, diff_w_qkv[j], diff_q_norm[j], diff_k_norm[j],
                                   diff_lambda_q1[j], diff_lambda_k1[j],
                                   diff_lambda_q2[j], diff_lambda_k2[j],
                                   diff_subln[j], diff_w_out[j], lambda_init, cos_a, sin_a)
        x = x + swiglu(rms_norm(x, norm_ffn[layer]), ffn_w_gate[layer], ffn_w_up[layer],
                       ffn_w_down[layer])
    return x
```

```python
import functools
import math

import jax
import jax.numpy as jnp
import numpy as np
from jax import lax
from jax.experimental import pallas as pl
from jax.experimental.pallas import tpu as pltpu

D_MODEL = 2048
DEPTH = 4
ROPE_THETA = 10000.0
NORM_EPS = 1e-6
RET_HEADS = 8
RET_DK = 64
RET_DV = 128
CONV_WIDTH = 1024
CONV_K = 3
DIFF_HEADS = 8
DIFF_HEAD_DIM = 128
FFN_HIDDEN = 5632
RET_QK = RET_HEADS * RET_DK
RET_V = RET_HEADS * RET_DV
HYB_IN = 2 * RET_QK + 2 * RET_V + 3 * CONV_WIDTH
DIFF_QKV = 6144
DIFF_QK_COLS = 2 * DIFF_HEADS * DIFF_HEAD_DIM

LANES = 128
SUBLANES = 8
VMEM_LIMIT_BYTES = 56 * 1024 * 1024

NORM_TM = 512
PROJ_TM = 1024
PROJ_TN = 512
OUT_TM = 512
FFN_TM = 512
FFN_TF = 512
RET_TS = 256
ATT_TQ = 512
ATT_TK = 512
MASK_VALUE = -1e30

F32 = jnp.float32
BF16 = jnp.bfloat16


def _compiler_params(semantics):
    return pltpu.CompilerParams(dimension_semantics=semantics,
                                vmem_limit_bytes=VMEM_LIMIT_BYTES)


def _rms_rows(v):
    return v * lax.rsqrt(jnp.mean(v * v, axis=-1, keepdims=True) + NORM_EPS)


def _norm_kernel(x_ref, g_ref, o_ref):
    o_ref[...] = (_rms_rows(x_ref[...]) * g_ref[...]).astype(o_ref.dtype)


def _rmsnorm(x, g):
    s, d = x.shape
    return pl.pallas_call(
        _norm_kernel,
        grid=(s // NORM_TM,),
        in_specs=[pl.BlockSpec((NORM_TM, d), lambda i: (i, 0)),
                  pl.BlockSpec((1, d), lambda i: (0, 0))],
        out_specs=pl.BlockSpec((NORM_TM, d), lambda i: (i, 0)),
        out_shape=jax.ShapeDtypeStruct((s, d), BF16),
        compiler_params=_compiler_params(("parallel",)),
        name="rmsnorm",
    )(x, g.reshape(1, d))


def _rotate_half(blk, half):
    if 2 * half == LANES:
        return pltpu.roll(blk, half, axis=1)
    lane = lax.broadcasted_iota(jnp.int32, blk.shape, 1)
    first = (lane % (2 * half)) < half
    return jnp.where(first, pltpu.roll(blk, LANES - half, axis=1),
                     pltpu.roll(blk, half, axis=1))


def _proj_kernel(x_ref, w_ref, cos_ref, sin_ref, gain_ref, o_ref, *,
                 rope_tiles, rope_half, qk_norm):
    acc = jnp.dot(x_ref[...], w_ref[...], preferred_element_type=F32)
    n = pl.program_id(1)

    @pl.when(n >= rope_tiles)
    def _():
        o_ref[...] = acc.astype(o_ref.dtype)

    @pl.when(n < rope_tiles)
    def _():
        cos = cos_ref[...]
        sin = sin_ref[...]
        for j in range(acc.shape[1] // LANES):
            cols = slice(j * LANES, (j + 1) * LANES)
            blk = acc[:, cols]
            if qk_norm:
                blk = _rms_rows(blk) * gain_ref[:, cols]
            out = blk * cos + _rotate_half(blk, rope_half) * sin
            o_ref[:, cols] = out.astype(o_ref.dtype)


def _project(xn, w, cos, sin, gain, *, rope_cols, rope_half, qk_norm):
    s, d = xn.shape
    n = w.shape[1]
    kernel = functools.partial(_proj_kernel, rope_tiles=rope_cols // PROJ_TN,
                               rope_half=rope_half, qk_norm=qk_norm)
    return pl.pallas_call(
        kernel,
        grid=(s // PROJ_TM, n // PROJ_TN),
        in_specs=[pl.BlockSpec((PROJ_TM, d), lambda m, j: (m, 0)),
                  pl.BlockSpec((d, PROJ_TN), lambda m, j: (0, j)),
                  pl.BlockSpec((PROJ_TM, LANES), lambda m, j: (m, 0)),
                  pl.BlockSpec((PROJ_TM, LANES), lambda m, j: (m, 0)),
                  pl.BlockSpec((1, PROJ_TN), lambda m, j: (0, j))],
        out_specs=pl.BlockSpec((PROJ_TM, PROJ_TN), lambda m, j: (m, j)),
        out_shape=jax.ShapeDtypeStruct((s, n), BF16),
        compiler_params=_compiler_params(("parallel", "arbitrary")),
        name="proj_rope_qknorm" if qk_norm else "proj_rope",
    )(xn, w, cos, sin, gain)


def _retention_tables(ts):
    h = np.arange(RET_HEADS, dtype=np.float64)
    log_g = np.log(1.0 - np.exp2(-5.0 - h))
    idx = np.arange(ts, dtype=np.float64)
    dist = idx[:, None] - idx[None, :]
    scale = RET_DK ** -0.5
    decay = np.where(dist >= 0, np.exp(log_g[:, None, None] * np.maximum(dist, 0.0)), 0.0) * scale
    q_dec = np.exp(log_g[:, None] * (idx + 1.0))
    k_dec = np.exp(log_g[:, None] * (ts - 1.0 - idx)) * scale
    chunk_dec = np.exp(log_g * ts)
    q_dec = np.broadcast_to(q_dec[:, :, None], (RET_HEADS, ts, LANES))
    k_dec = np.broadcast_to(k_dec[:, :, None], (RET_HEADS, ts, LANES))
    return (jnp.asarray(decay, F32), jnp.asarray(q_dec, F32), jnp.asarray(k_dec, F32),
            [float(np.float32(c)) for c in chunk_dec])


def _silu(v):
    return v / (1.0 + jnp.exp(-v))


def _hybrid_kernel(q_ref, k_ref, v_ref, g_ref, cb_ref, cc_ref, cx_ref, convw_ref,
                   decay_ref, qdec_ref, kdec_ref, o_ref, state_ref, u_ref, *, chunk_dec):
    ts = q_ref.shape[0]

    @pl.when(pl.program_id(0) == 0)
    def _():
        state_ref[...] = jnp.zeros_like(state_ref)
        u_ref[0:SUBLANES, :] = jnp.zeros((SUBLANES, u_ref.shape[1]), F32)

    lane = lax.broadcasted_iota(jnp.int32, (ts, LANES), 1)
    for h in range(RET_HEADS):
        pair = slice((h // 2) * LANES, (h // 2 + 1) * LANES)
        head = slice(h * RET_DV, (h + 1) * RET_DV)
        in_head = (lane // RET_DK) == (h % 2)
        qm = jnp.where(in_head, q_ref[:, pair].astype(F32), 0.0)
        kp = k_ref[:, pair]
        vh = v_ref[:, head]
        s = lax.dot_general(qm.astype(BF16), kp, (((1,), (1,)), ((), ())),
                            preferred_element_type=F32)
        inner = (s * decay_ref[h]).astype(BF16)
        qd = (qm * qdec_ref[h]).astype(BF16)
        o = (jnp.dot(inner, vh, preferred_element_type=F32)
             + jnp.dot(qd, state_ref[h].astype(BF16), preferred_element_type=F32))
        kd = (kp.astype(F32) * kdec_ref[h]).astype(BF16)
        state_ref[h] = chunk_dec[h] * state_ref[h] + lax.dot_general(
            kd, vh, (((0,), (0,)), ((), ())), preferred_element_type=F32)
        gate = _silu(g_ref[:, head].astype(F32))
        o_ref[:, head] = (gate * _rms_rows(o)).astype(o_ref.dtype)

    u_ref[SUBLANES:SUBLANES + ts, :] = cc_ref[...].astype(F32) * cx_ref[...].astype(F32)
    y = (convw_ref[0:1, :] * u_ref[SUBLANES - 2:SUBLANES - 2 + ts, :]
         + convw_ref[1:2, :] * u_ref[SUBLANES - 1:SUBLANES - 1 + ts, :]
         + convw_ref[2:3, :] * u_ref[SUBLANES:SUBLANES + ts, :])
    o_ref[:, RET_V:] = (cb_ref[...].astype(F32) * y).astype(o_ref.dtype)
    u_ref[0:SUBLANES, :] = u_ref[ts:ts + SUBLANES, :]


def _hybrid_mix(proj, conv_w):
    s = proj.shape[0]
    ts = RET_TS
    decay, q_dec, k_dec, chunk_dec = _retention_tables(ts)
    qk_blk = RET_QK
    w_blk = RET_V
    const3 = lambda i: (0, 0, 0)
    kernel = functools.partial(_hybrid_kernel, chunk_dec=chunk_dec)
    return pl.pallas_call(
        kernel,
        grid=(s // ts,),
        in_specs=[pl.BlockSpec((ts, qk_blk), lambda i: (i, 0)),
                  pl.BlockSpec((ts, qk_blk), lambda i: (i, 1)),
                  pl.BlockSpec((ts, w_blk), lambda i: (i, 1)),
                  pl.BlockSpec((ts, w_blk), lambda i: (i, 2)),
                  pl.BlockSpec((ts, w_blk), lambda i: (i, 3)),
                  pl.BlockSpec((ts, w_blk), lambda i: (i, 4)),
                  pl.BlockSpec((ts, w_blk), lambda i: (i, 5)),
                  pl.BlockSpec((SUBLANES, CONV_WIDTH), lambda i: (0, 0)),
                  pl.BlockSpec((RET_HEADS, ts, ts), const3),
                  pl.BlockSpec((RET_HEADS, ts, LANES), const3),
                  pl.BlockSpec((RET_HEADS, ts, LANES), const3)],
        out_specs=pl.BlockSpec((ts, RET_V + CONV_WIDTH), lambda i: (i, 0)),
        out_shape=jax.ShapeDtypeStruct((s, RET_V + CONV_WIDTH), BF16),
        scratch_shapes=[pltpu.VMEM((RET_HEADS, LANES, RET_DV), F32),
                        pltpu.VMEM((SUBLANES + ts, CONV_WIDTH), F32)],
        compiler_params=_compiler_params(("arbitrary",)),
        name="retention_conv",
    )(proj, proj, proj, proj, proj, proj, proj,
      jnp.pad(conv_w, ((0, SUBLANES - CONV_K), (0, 0))), decay, q_dec, k_dec)


def _attn_kernel(q_ref, k_ref, v_ref, lq1_ref, lk1_ref, lq2_ref, lk2_ref, subln_ref,
                 o_ref, m_ref, l_ref, acc_ref, *, lambda_init):
    tq = q_ref.shape[0]
    tk = ATT_TK
    d = DIFF_HEAD_DIM
    qi = pl.program_id(1)

    m_ref[...] = jnp.full_like(m_ref, MASK_VALUE)
    l_ref[...] = jnp.zeros_like(l_ref)
    acc_ref[...] = jnp.zeros_like(acc_ref)

    def update(j, masked):
        start = pl.multiple_of(j * tk, tk)
        k = k_ref[pl.ds(start, tk), :]
        v = v_ref[pl.ds(start, tk), :]
        for i in range(2):
            s = lax.dot_general(q_ref[:, i * d:(i + 1) * d], k[:, i * d:(i + 1) * d],
                                (((1,), (1,)), ((), ())), preferred_element_type=F32)
            if masked:
                row = lax.broadcasted_iota(jnp.int32, s.shape, 0)
                col = lax.broadcasted_iota(jnp.int32, s.shape, 1)
                s = jnp.where(col <= row, s, MASK_VALUE)
            m_prev = m_ref[i]
            m_new = jnp.maximum(m_prev, jnp.max(s, axis=1, keepdims=True))
            alpha = jnp.exp(m_prev - m_new)
            p = jnp.exp(s - m_new[:, :1])
            l_ref[i] = alpha * l_ref[i] + jnp.sum(p, axis=1, keepdims=True)
            acc_ref[i] = alpha[:, :1] * acc_ref[i] + jnp.dot(
                p.astype(BF16), v, preferred_element_type=F32)
            m_ref[i] = m_new

    def body(j, carry):
        update(j, masked=False)
        return carry

    lax.fori_loop(0, qi, body, 0)
    update(qi, masked=True)

    lam = (jnp.exp(jnp.sum(lq1_ref[...] * lk1_ref[...], axis=1, keepdims=True))
           - jnp.exp(jnp.sum(lq2_ref[...] * lk2_ref[...], axis=1, keepdims=True))
           + lambda_init)
    o = acc_ref[0] / l_ref[0][:, :1] - lam * (acc_ref[1] / l_ref[1][:, :1])
    o = _rms_rows(o) * subln_ref[...] * (1.0 - lambda_init)
    o_ref[...] = o.astype(o_ref.dtype)


def _diff_attention(qkv, lq1, lk1, lq2, lk2, subln, lambda_init):
    s = qkv.shape[0]
    hd = 2 * DIFF_HEAD_DIM
    assert ATT_TQ == ATT_TK
    vec = lambda a: a.reshape(1, -1).astype(F32)
    small = lambda w: pl.BlockSpec((1, w), lambda h, i: (0, 0))
    kernel = functools.partial(_attn_kernel, lambda_init=lambda_init)
    return pl.pallas_call(
        kernel,
        grid=(DIFF_HEADS, s // ATT_TQ),
        in_specs=[pl.BlockSpec((ATT_TQ, hd), lambda h, i: (i, h)),
                  pl.BlockSpec((s, hd), lambda h, i: (0, DIFF_HEADS + h)),
                  pl.BlockSpec((s, hd), lambda h, i: (0, 2 * DIFF_HEADS + h)),
                  small(DIFF_HEAD_DIM), small(DIFF_HEAD_DIM),
                  small(DIFF_HEAD_DIM), small(DIFF_HEAD_DIM), small(hd)],
        out_specs=pl.BlockSpec((ATT_TQ, hd), lambda h, i: (i, h)),
        out_shape=jax.ShapeDtypeStruct((s, DIFF_HEADS * hd), BF16),
        scratch_shapes=[pltpu.VMEM((2, ATT_TQ, LANES), F32),
                        pltpu.VMEM((2, ATT_TQ, LANES), F32),
                        pltpu.VMEM((2, ATT_TQ, hd), F32)],
        compiler_params=_compiler_params(("parallel", "arbitrary")),
        name="diff_attention",
    )(qkv, qkv, qkv, vec(lq1), vec(lk1), vec(lq2), vec(lk2), vec(subln))


def _out_kernel(a_ref, w_ref, x_ref, g_ref, xo_ref, xn_ref):
    x = x_ref[...] + jnp.dot(a_ref[...], w_ref[...], preferred_element_type=F32)
    xo_ref[...] = x
    xn_ref[...] = (_rms_rows(x) * g_ref[...]).astype(xn_ref.dtype)


def _out_project(a, w, x, g):
    s, d = x.shape
    row = lambda i: (i, 0)
    return pl.pallas_call(
        _out_kernel,
        grid=(s // OUT_TM,),
        in_specs=[pl.BlockSpec((OUT_TM, a.shape[1]), row),
                  pl.BlockSpec(w.shape, lambda i: (0, 0)),
                  pl.BlockSpec((OUT_TM, d), row),
                  pl.BlockSpec((1, d), lambda i: (0, 0))],
        out_specs=[pl.BlockSpec((OUT_TM, d), row), pl.BlockSpec((OUT_TM, d), row)],
        out_shape=[jax.ShapeDtypeStruct((s, d), F32), jax.ShapeDtypeStruct((s, d), BF16)],
        compiler_params=_compiler_params(("parallel",)),
        name="out_proj_residual_norm",
    )(a, w, x, g.reshape(1, d))


def _ffn_kernel(xn_ref, wg_ref, wu_ref, wd_ref, x_ref, *rest):
    if len(rest) == 4:
        g_ref, xo_ref, xn_out_ref, acc_ref = rest
    else:
        (xo_ref, acc_ref), g_ref, xn_out_ref = rest, None, None
    f = pl.program_id(1)
    xn = xn_ref[...]
    gate = jnp.dot(xn, wg_ref[...], preferred_element_type=F32)
    up = jnp.dot(xn, wu_ref[...], preferred_element_type=F32)
    act = (_silu(gate) * up).astype(BF16)
    part = jnp.dot(act, wd_ref[...], preferred_element_type=F32)

    @pl.when(f == 0)
    def _():
        acc_ref[...] = part

    @pl.when(f > 0)
    def _():
        acc_ref[...] += part

    @pl.when(f == pl.num_programs(1) - 1)
    def _():
        x = x_ref[...] + acc_ref[...]
        xo_ref[...] = x
        if xn_out_ref is not None:
            xn_out_ref[...] = (_rms_rows(x) * g_ref[...]).astype(xn_out_ref.dtype)


def _ffn(xn, wg, wu, wd, x, g):
    s, d = x.shape
    hidden = wg.shape[1]
    row = lambda i, f: (i, 0)
    in_specs = [pl.BlockSpec((FFN_TM, d), row),
                pl.BlockSpec((d, FFN_TF), lambda i, f: (0, f)),
                pl.BlockSpec((d, FFN_TF), lambda i, f: (0, f)),
                pl.BlockSpec((FFN_TF, d), lambda i, f: (f, 0)),
                pl.BlockSpec((FFN_TM, d), row)]
    out_specs = [pl.BlockSpec((FFN_TM, d), row)]
    out_shape = [jax.ShapeDtypeStruct((s, d), F32)]
    args = [xn, wg, wu, wd, x]
    if g is not None:
        in_specs.append(pl.BlockSpec((1, d), lambda i, f: (0, 0)))
        out_specs.append(pl.BlockSpec((FFN_TM, d), row))
        out_shape.append(jax.ShapeDtypeStruct((s, d), BF16))
        args.append(g.reshape(1, d))
    outs = pl.pallas_call(
        _ffn_kernel,
        grid=(s // FFN_TM, hidden // FFN_TF),
        in_specs=in_specs,
        out_specs=out_specs,
        out_shape=out_shape,
        scratch_shapes=[pltpu.VMEM((FFN_TM, d), F32)],
        compiler_params=_compiler_params(("parallel", "arbitrary")),
        name="swiglu_ffn_residual_norm" if g is not None else "swiglu_ffn_residual",
    )(*args)
    return (outs[0], outs[1]) if g is not None else (outs[0], None)


def _rope_tables(seq, dim):
    inv = ROPE_THETA ** (-jnp.arange(0, dim, 2, dtype=F32) / dim)
    ang = jnp.arange(seq, dtype=F32)[:, None] * inv[None, :]
    cos = jnp.cos(ang)
    sin = jnp.sin(ang)
    reps = LANES // dim
    cos_l = jnp.tile(jnp.concatenate([cos, cos], axis=1), (1, reps))
    sin_l = jnp.tile(jnp.concatenate([-sin, sin], axis=1), (1, reps))
    return cos_l, sin_l


def kernel(x, norm_mix, norm_ffn, hyb_w_in, hyb_conv_w, hyb_w_out, diff_w_qkv, diff_q_norm,
           diff_k_norm, diff_lambda_q1, diff_lambda_k1, diff_lambda_q2, diff_lambda_k2,
           diff_subln, diff_w_out, ffn_w_gate, ffn_w_up, ffn_w_down):
    b, s, d = x.shape
    assert b == 1 and d == D_MODEL
    xs = x.reshape(s, d)
    cos_r, sin_r = _rope_tables(s, RET_DK)
    cos_a, sin_a = _rope_tables(s, DIFF_HEAD_DIM)
    ones_gain = jnp.ones((1, HYB_IN), F32)

    xn = _rmsnorm(xs, norm_mix[0])
    for layer in range(DEPTH):
        j = layer // 2
        if layer % 2 == 0:
            proj = _project(xn, hyb_w_in[j].astype(BF16), cos_r, sin_r, ones_gain,
                            rope_cols=2 * RET_QK, rope_half=RET_DK // 2, qk_norm=False)
            mixed = _hybrid_mix(proj, hyb_conv_w[j])
            w_out = hyb_w_out[j]
        else:
            lambda_init = 0.8 - 0.6 * math.exp(-0.3 * layer)
            n_groups = DIFF_QK_COLS // DIFF_HEAD_DIM
            gain = jnp.concatenate([
                jnp.tile(diff_q_norm[j] * DIFF_HEAD_DIM ** -0.5, n_groups),
                jnp.tile(diff_k_norm[j], n_groups),
                jnp.ones((DIFF_QKV - 2 * DIFF_QK_COLS,), F32)]).reshape(1, DIFF_QKV)
            qkv = _project(xn, diff_w_qkv[j].astype(BF16), cos_a, sin_a, gain,
                           rope_cols=2 * DIFF_QK_COLS, rope_half=DIFF_HEAD_DIM // 2,
                           qk_norm=True)
            mixed = _diff_attention(qkv, diff_lambda_q1[j], diff_lambda_k1[j],
                                    diff_lambda_q2[j], diff_lambda_k2[j], diff_subln[j],
                                    lambda_init)
            w_out = diff_w_out[j]
        xs, xn = _out_project(mixed, w_out.astype(BF16), xs, norm_ffn[layer])
        g_next = norm_mix[layer + 1] if layer + 1 < DEPTH else None
        xs, xn = _ffn(xn, ffn_w_gate[layer].astype(BF16), ffn_w_up[layer].astype(BF16),
                      ffn_w_down[layer].astype(BF16), xs, g_next)
    return xs.reshape(b, s, d)
```

```python
import functools
import math

import jax
import jax.numpy as jnp
import numpy as np
from jax import lax
from jax.experimental import pallas as pl
from jax.experimental.pallas import tpu as pltpu

D_MODEL = 2048
DEPTH = 4
ROPE_THETA = 10000.0
NORM_EPS = 1e-6
RET_HEADS = 8
RET_DK = 64
RET_DV = 128
CONV_WIDTH = 1024
CONV_K = 3
DIFF_HEADS = 8
DIFF_HEAD_DIM = 128
FFN_HIDDEN = 5632
RET_QK = RET_HEADS * RET_DK
RET_V = RET_HEADS * RET_DV
HYB_IN = 2 * RET_QK + 2 * RET_V + 3 * CONV_WIDTH
DIFF_QKV = 6144
DIFF_QK_COLS = 2 * DIFF_HEADS * DIFF_HEAD_DIM

LANES = 128
SUBLANES = 8
VMEM_LIMIT_BYTES = 56 * 1024 * 1024

NORM_TM = 512
PROJ_TM = 1024
PROJ_TN = 512
OUT_TM = 512
FFN_TM = 512
FFN_TF = 512
RET_TS = 256
ATT_TQ = 512
ATT_TK = 512
MASK_VALUE = -1e30
LOG2_E = math.log2(math.e)
ATT_BOUND_SLACK = 1.02
ATT_BOUND_MAX = 60.0

F32 = jnp.float32
BF16 = jnp.bfloat16


def _compiler_params(semantics):
    return pltpu.CompilerParams(dimension_semantics=semantics,
                                vmem_limit_bytes=VMEM_LIMIT_BYTES)


def _rms_rows(v):
    return v * lax.rsqrt(jnp.mean(v * v, axis=-1, keepdims=True) + NORM_EPS)


def _norm_kernel(x_ref, g_ref, o_ref):
    o_ref[...] = (_rms_rows(x_ref[...]) * g_ref[...]).astype(o_ref.dtype)


def _rmsnorm(x, g):
    s, d = x.shape
    return pl.pallas_call(
        _norm_kernel,
        grid=(s // NORM_TM,),
        in_specs=[pl.BlockSpec((NORM_TM, d), lambda i: (i, 0)),
                  pl.BlockSpec((1, d), lambda i: (0, 0))],
        out_specs=pl.BlockSpec((NORM_TM, d), lambda i: (i, 0)),
        out_shape=jax.ShapeDtypeStruct((s, d), BF16),
        compiler_params=_compiler_params(("parallel",)),
        name="rmsnorm",
    )(x, g.reshape(1, d))


def _rotate_half(blk, half):
    if 2 * half == LANES:
        return pltpu.roll(blk, half, axis=1)
    lane = lax.broadcasted_iota(jnp.int32, blk.shape, 1)
    first = (lane % (2 * half)) < half
    return jnp.where(first, pltpu.roll(blk, LANES - half, axis=1),
                     pltpu.roll(blk, half, axis=1))


def _proj_kernel(x_ref, w_ref, cos_ref, sin_ref, gain_ref, o_ref, *,
                 rope_tiles, rope_half, qk_norm):
    acc = jnp.dot(x_ref[...], w_ref[...], preferred_element_type=F32)
    n = pl.program_id(1)

    @pl.when(n >= rope_tiles)
    def _():
        o_ref[...] = acc.astype(o_ref.dtype)

    @pl.when(n < rope_tiles)
    def _():
        cos = cos_ref[...]
        sin = sin_ref[...]
        for j in range(acc.shape[1] // LANES):
            cols = slice(j * LANES, (j + 1) * LANES)
            blk = acc[:, cols]
            if qk_norm:
                blk = _rms_rows(blk) * gain_ref[:, cols]
            out = blk * cos + _rotate_half(blk, rope_half) * sin
            o_ref[:, cols] = out.astype(o_ref.dtype)


def _project(xn, w, cos, sin, gain, *, rope_cols, rope_half, qk_norm):
    s, d = xn.shape
    n = w.shape[1]
    kernel = functools.partial(_proj_kernel, rope_tiles=rope_cols // PROJ_TN,
                               rope_half=rope_half, qk_norm=qk_norm)
    return pl.pallas_call(
        kernel,
        grid=(s // PROJ_TM, n // PROJ_TN),
        in_specs=[pl.BlockSpec((PROJ_TM, d), lambda m, j: (m, 0)),
                  pl.BlockSpec((d, PROJ_TN), lambda m, j: (0, j)),
                  pl.BlockSpec((PROJ_TM, LANES), lambda m, j: (m, 0)),
                  pl.BlockSpec((PROJ_TM, LANES), lambda m, j: (m, 0)),
                  pl.BlockSpec((1, PROJ_TN), lambda m, j: (0, j))],
        out_specs=pl.BlockSpec((PROJ_TM, PROJ_TN), lambda m, j: (m, j)),
        out_shape=jax.ShapeDtypeStruct((s, n), BF16),
        compiler_params=_compiler_params(("parallel", "arbitrary")),
        name="proj_rope_qknorm" if qk_norm else "proj_rope",
    )(xn, w, cos, sin, gain)


def _retention_tables(ts):
    h = np.arange(RET_HEADS, dtype=np.float64)
    log_g = np.log(1.0 - np.exp2(-5.0 - h))
    idx = np.arange(ts, dtype=np.float64)
    dist = idx[:, None] - idx[None, :]
    scale = RET_DK ** -0.5
    decay = np.where(dist >= 0, np.exp(log_g[:, None, None] * np.maximum(dist, 0.0)), 0.0) * scale
    q_dec = np.exp(log_g[:, None] * (idx + 1.0))
    k_dec = np.exp(log_g[:, None] * (ts - 1.0 - idx)) * scale
    chunk_dec = np.exp(log_g * ts)
    q_dec = np.broadcast_to(q_dec[:, :, None], (RET_HEADS, ts, LANES))
    k_dec = np.broadcast_to(k_dec[:, :, None], (RET_HEADS, ts, LANES))
    return (jnp.asarray(decay, F32), jnp.asarray(q_dec, F32), jnp.asarray(k_dec, F32),
            [float(np.float32(c)) for c in chunk_dec])


def _silu(v):
    return v / (1.0 + jnp.exp(-v))


def _hybrid_kernel(q_ref, k_ref, v_ref, g_ref, cb_ref, cc_ref, cx_ref, convw_ref,
                   decay_ref, qdec_ref, kdec_ref, o_ref, state_ref, u_ref, *, chunk_dec):
    ts = q_ref.shape[0]

    @pl.when(pl.program_id(0) == 0)
    def _():
        state_ref[...] = jnp.zeros_like(state_ref)
        u_ref[0:SUBLANES, :] = jnp.zeros((SUBLANES, u_ref.shape[1]), F32)

    lane = lax.broadcasted_iota(jnp.int32, (ts, LANES), 1)
    for h in range(RET_HEADS):
        pair = slice((h // 2) * LANES, (h // 2 + 1) * LANES)
        head = slice(h * RET_DV, (h + 1) * RET_DV)
        in_head = (lane // RET_DK) == (h % 2)
        qm = jnp.where(in_head, q_ref[:, pair].astype(F32), 0.0)
        kp = k_ref[:, pair]
        vh = v_ref[:, head]
        s = lax.dot_general(qm.astype(BF16), kp, (((1,), (1,)), ((), ())),
                            preferred_element_type=F32)
        inner = (s * decay_ref[h]).astype(BF16)
        qd = (qm * qdec_ref[h]).astype(BF16)
        o = (jnp.dot(inner, vh, preferred_element_type=F32)
             + jnp.dot(qd, state_ref[h].astype(BF16), preferred_element_type=F32))
        kd = (kp.astype(F32) * kdec_ref[h]).astype(BF16)
        state_ref[h] = chunk_dec[h] * state_ref[h] + lax.dot_general(
            kd, vh, (((0,), (0,)), ((), ())), preferred_element_type=F32)
        gate = _silu(g_ref[:, head].astype(F32))
        o_ref[:, head] = (gate * _rms_rows(o)).astype(o_ref.dtype)

    u_ref[SUBLANES:SUBLANES + ts, :] = cc_ref[...].astype(F32) * cx_ref[...].astype(F32)
    y = (convw_ref[0:1, :] * u_ref[SUBLANES - 2:SUBLANES - 2 + ts, :]
         + convw_ref[1:2, :] * u_ref[SUBLANES - 1:SUBLANES - 1 + ts, :]
         + convw_ref[2:3, :] * u_ref[SUBLANES:SUBLANES + ts, :])
    o_ref[:, RET_V:] = (cb_ref[...].astype(F32) * y).astype(o_ref.dtype)
    u_ref[0:SUBLANES, :] = u_ref[ts:ts + SUBLANES, :]


def _hybrid_mix(proj, conv_w):
    s = proj.shape[0]
    ts = RET_TS
    decay, q_dec, k_dec, chunk_dec = _retention_tables(ts)
    qk_blk = RET_QK
    w_blk = RET_V
    const3 = lambda i: (0, 0, 0)
    kernel = functools.partial(_hybrid_kernel, chunk_dec=chunk_dec)
    return pl.pallas_call(
        kernel,
        grid=(s // ts,),
        in_specs=[pl.BlockSpec((ts, qk_blk), lambda i: (i, 0)),
                  pl.BlockSpec((ts, qk_blk), lambda i: (i, 1)),
                  pl.BlockSpec((ts, w_blk), lambda i: (i, 1)),
                  pl.BlockSpec((ts, w_blk), lambda i: (i, 2)),
                  pl.BlockSpec((ts, w_blk), lambda i: (i, 3)),
                  pl.BlockSpec((ts, w_blk), lambda i: (i, 4)),
                  pl.BlockSpec((ts, w_blk), lambda i: (i, 5)),
                  pl.BlockSpec((SUBLANES, CONV_WIDTH), lambda i: (0, 0)),
                  pl.BlockSpec((RET_HEADS, ts, ts), const3),
                  pl.BlockSpec((RET_HEADS, ts, LANES), const3),
                  pl.BlockSpec((RET_HEADS, ts, LANES), const3)],
        out_specs=pl.BlockSpec((ts, RET_V + CONV_WIDTH), lambda i: (i, 0)),
        out_shape=jax.ShapeDtypeStruct((s, RET_V + CONV_WIDTH), BF16),
        scratch_shapes=[pltpu.VMEM((RET_HEADS, LANES, RET_DV), F32),
                        pltpu.VMEM((SUBLANES + ts, CONV_WIDTH), F32)],
        compiler_params=_compiler_params(("arbitrary",)),
        name="retention_conv",
    )(proj, proj, proj, proj, proj, proj, proj,
      jnp.pad(conv_w, ((0, SUBLANES - CONV_K), (0, 0))), decay, q_dec, k_dec)


def _lane_partial_sum(p):
    total = p[:, 0:LANES]
    for t in range(1, p.shape[1] // LANES):
        total = total + p[:, t * LANES:(t + 1) * LANES]
    return total


def _attn_kernel(bound_ref, q_ref, k_ref, v_ref, lq1_ref, lk1_ref, lq2_ref, lk2_ref,
                 subln_ref, o_ref, m_ref, l_ref, acc_ref, *, lambda_init):
    tk = ATT_TK
    d = DIFF_HEAD_DIM
    qi = pl.program_id(1)
    bound = bound_ref[0]

    l_ref[...] = jnp.zeros_like(l_ref)
    acc_ref[...] = jnp.zeros_like(acc_ref)

    def scores(j, i, masked):
        start = pl.multiple_of(j * tk, tk)
        s = lax.dot_general(q_ref[:, i * d:(i + 1) * d],
                            k_ref[pl.ds(start, tk), i * d:(i + 1) * d],
                            (((1,), (1,)), ((), ())), preferred_element_type=F32)
        if masked:
            row = lax.broadcasted_iota(jnp.int32, s.shape, 0)
            col = lax.broadcasted_iota(jnp.int32, s.shape, 1)
            s = jnp.where(col <= row, s, MASK_VALUE)
        return s, v_ref[pl.ds(start, tk), :]

    def update_bounded(j, masked):
        for i in range(2):
            s, v = scores(j, i, masked)
            p = jnp.exp2(s - bound)
            l_ref[i] += _lane_partial_sum(p)
            acc_ref[i] += jnp.dot(p.astype(BF16), v, preferred_element_type=F32)

    def update_online(j, masked):
        for i in range(2):
            s, v = scores(j, i, masked)
            m_prev = m_ref[i]
            m_new = jnp.maximum(m_prev, jnp.max(s, axis=1, keepdims=True))
            alpha = jnp.exp2(m_prev - m_new)
            p = jnp.exp2(s - m_new[:, :1])
            l_ref[i] = alpha * l_ref[i] + _lane_partial_sum(p)
            acc_ref[i] = alpha[:, :1] * acc_ref[i] + jnp.dot(
                p.astype(BF16), v, preferred_element_type=F32)
            m_ref[i] = m_new

    def run(update):
        def body(j, carry):
            update(j, masked=False)
            return carry
        lax.fori_loop(0, qi, body, 0)
        update(qi, masked=True)

    @pl.when(bound <= ATT_BOUND_MAX)
    def _():
        run(update_bounded)

    @pl.when(bound > ATT_BOUND_MAX)
    def _():
        m_ref[...] = jnp.full_like(m_ref, MASK_VALUE)
        run(update_online)

    lam = (jnp.exp(jnp.sum(lq1_ref[...] * lk1_ref[...], axis=1, keepdims=True))
           - jnp.exp(jnp.sum(lq2_ref[...] * lk2_ref[...], axis=1, keepdims=True))
           + lambda_init)
    l0 = jnp.sum(l_ref[0], axis=1, keepdims=True)
    l1 = jnp.sum(l_ref[1], axis=1, keepdims=True)
    o = acc_ref[0] / l0 - lam * (acc_ref[1] / l1)
    o = _rms_rows(o) * subln_ref[...] * (1.0 - lambda_init)
    o_ref[...] = o.astype(o_ref.dtype)


def _diff_attention(qkv, score_bound, lq1, lk1, lq2, lk2, subln, lambda_init):
    s = qkv.shape[0]
    hd = 2 * DIFF_HEAD_DIM
    assert ATT_TQ == ATT_TK
    vec = lambda a: a.reshape(1, -1).astype(F32)
    small = lambda w: pl.BlockSpec((1, w), lambda h, i: (0, 0))
    kernel = functools.partial(_attn_kernel, lambda_init=lambda_init)
    return pl.pallas_call(
        kernel,
        grid=(DIFF_HEADS, s // ATT_TQ),
        in_specs=[pl.BlockSpec(memory_space=pltpu.SMEM),
                  pl.BlockSpec((ATT_TQ, hd), lambda h, i: (i, h)),
                  pl.BlockSpec((s, hd), lambda h, i: (0, DIFF_HEADS + h)),
                  pl.BlockSpec((s, hd), lambda h, i: (0, 2 * DIFF_HEADS + h)),
                  small(DIFF_HEAD_DIM), small(DIFF_HEAD_DIM),
                  small(DIFF_HEAD_DIM), small(DIFF_HEAD_DIM), small(hd)],
        out_specs=pl.BlockSpec((ATT_TQ, hd), lambda h, i: (i, h)),
        out_shape=jax.ShapeDtypeStruct((s, DIFF_HEADS * hd), BF16),
        scratch_shapes=[pltpu.VMEM((2, ATT_TQ, LANES), F32),
                        pltpu.VMEM((2, ATT_TQ, LANES), F32),
                        pltpu.VMEM((2, ATT_TQ, hd), F32)],
        compiler_params=_compiler_params(("parallel", "arbitrary")),
        name="diff_attention",
    )(score_bound, qkv, qkv, qkv, vec(lq1), vec(lk1), vec(lq2), vec(lk2), vec(subln))


def _out_kernel(a_ref, w_ref, x_ref, g_ref, xo_ref, xn_ref):
    x = x_ref[...] + jnp.dot(a_ref[...], w_ref[...], preferred_element_type=F32)
    xo_ref[...] = x
    xn_ref[...] = (_rms_rows(x) * g_ref[...]).astype(xn_ref.dtype)


def _out_project(a, w, x, g):
    s, d = x.shape
    row = lambda i: (i, 0)
    return pl.pallas_call(
        _out_kernel,
        grid=(s // OUT_TM,),
        in_specs=[pl.BlockSpec((OUT_TM, a.shape[1]), row),
                  pl.BlockSpec(w.shape, lambda i: (0, 0)),
                  pl.BlockSpec((OUT_TM, d), row),
                  pl.BlockSpec((1, d), lambda i: (0, 0))],
        out_specs=[pl.BlockSpec((OUT_TM, d), row), pl.BlockSpec((OUT_TM, d), row)],
        out_shape=[jax.ShapeDtypeStruct((s, d), F32), jax.ShapeDtypeStruct((s, d), BF16)],
        compiler_params=_compiler_params(("parallel",)),
        name="out_proj_residual_norm",
    )(a, w, x, g.reshape(1, d))


def _ffn_kernel(xn_ref, wg_ref, wu_ref, wd_ref, x_ref, *rest):
    if len(rest) == 4:
        g_ref, xo_ref, xn_out_ref, acc_ref = rest
    else:
        (xo_ref, acc_ref), g_ref, xn_out_ref = rest, None, None
    f = pl.program_id(1)
    xn = xn_ref[...]
    gate = jnp.dot(xn, wg_ref[...], preferred_element_type=F32)
    up = jnp.dot(xn, wu_ref[...], preferred_element_type=F32)
    act = (_silu(gate) * up).astype(BF16)
    part = jnp.dot(act, wd_ref[...], preferred_element_type=F32)

    @pl.when(f == 0)
    def _():
        acc_ref[...] = part

    @pl.when(f > 0)
    def _():
        acc_ref[...] += part

    @pl.when(f == pl.num_programs(1) - 1)
    def _():
        x = x_ref[...] + acc_ref[...]
        xo_ref[...] = x
        if xn_out_ref is not None:
            xn_out_ref[...] = (_rms_rows(x) * g_ref[...]).astype(xn_out_ref.dtype)


def _ffn(xn, wg, wu, wd, x, g):
    s, d = x.shape
    hidden = wg.shape[1]
    row = lambda i, f: (i, 0)
    in_specs = [pl.BlockSpec((FFN_TM, d), row),
                pl.BlockSpec((d, FFN_TF), lambda i, f: (0, f)),
                pl.BlockSpec((d, FFN_TF), lambda i, f: (0, f)),
                pl.BlockSpec((FFN_TF, d), lambda i, f: (f, 0)),
                pl.BlockSpec((FFN_TM, d), row)]
    out_specs = [pl.BlockSpec((FFN_TM, d), row)]
    out_shape = [jax.ShapeDtypeStruct((s, d), F32)]
    args = [xn, wg, wu, wd, x]
    if g is not None:
        in_specs.append(pl.BlockSpec((1, d), lambda i, f: (0, 0)))
        out_specs.append(pl.BlockSpec((FFN_TM, d), row))
        out_shape.append(jax.ShapeDtypeStruct((s, d), BF16))
        args.append(g.reshape(1, d))
    outs = pl.pallas_call(
        _ffn_kernel,
        grid=(s // FFN_TM, hidden // FFN_TF),
        in_specs=in_specs,
        out_specs=out_specs,
        out_shape=out_shape,
        scratch_shapes=[pltpu.VMEM((FFN_TM, d), F32)],
        compiler_params=_compiler_params(("parallel", "arbitrary")),
        name="swiglu_ffn_residual_norm" if g is not None else "swiglu_ffn_residual",
    )(*args)
    return (outs[0], outs[1]) if g is not None else (outs[0], None)


def _rope_tables(seq, dim):
    inv = ROPE_THETA ** (-jnp.arange(0, dim, 2, dtype=F32) / dim)
    ang = jnp.arange(seq, dtype=F32)[:, None] * inv[None, :]
    cos = jnp.cos(ang)
    sin = jnp.sin(ang)
    reps = LANES // dim
    cos_l = jnp.tile(jnp.concatenate([cos, cos], axis=1), (1, reps))
    sin_l = jnp.tile(jnp.concatenate([-sin, sin], axis=1), (1, reps))
    return cos_l, sin_l


def kernel(x, norm_mix, norm_ffn, hyb_w_in, hyb_conv_w, hyb_w_out, diff_w_qkv, diff_q_norm,
           diff_k_norm, diff_lambda_q1, diff_lambda_k1, diff_lambda_q2, diff_lambda_k2,
           diff_subln, diff_w_out, ffn_w_gate, ffn_w_up, ffn_w_down):
    b, s, d = x.shape
    assert b == 1 and d == D_MODEL
    xs = x.reshape(s, d)
    cos_r, sin_r = _rope_tables(s, RET_DK)
    cos_a, sin_a = _rope_tables(s, DIFF_HEAD_DIM)
    ones_gain = jnp.ones((1, HYB_IN), F32)

    xn = _rmsnorm(xs, norm_mix[0])
    for layer in range(DEPTH):
        j = layer // 2
        if layer % 2 == 0:
            proj = _project(xn, hyb_w_in[j].astype(BF16), cos_r, sin_r, ones_gain,
                            rope_cols=2 * RET_QK, rope_half=RET_DK // 2, qk_norm=False)
            mixed = _hybrid_mix(proj, hyb_conv_w[j])
            w_out = hyb_w_out[j]
        else:
            lambda_init = 0.8 - 0.6 * math.exp(-0.3 * layer)
            n_groups = DIFF_QK_COLS // DIFF_HEAD_DIM
            q_gain = diff_q_norm[j] * (LOG2_E * DIFF_HEAD_DIM ** -0.5)
            gain = jnp.concatenate([
                jnp.tile(q_gain, n_groups),
                jnp.tile(diff_k_norm[j], n_groups),
                jnp.ones((DIFF_QKV - 2 * DIFF_QK_COLS,), F32)]).reshape(1, DIFF_QKV)
            qkv = _project(xn, diff_w_qkv[j].astype(BF16), cos_a, sin_a, gain,
                           rope_cols=2 * DIFF_QK_COLS, rope_half=DIFF_HEAD_DIM // 2,
                           qk_norm=True)
            score_bound = (ATT_BOUND_SLACK * DIFF_HEAD_DIM * jnp.max(jnp.abs(q_gain))
                           * jnp.max(jnp.abs(diff_k_norm[j]))).reshape(1).astype(F32)
            mixed = _diff_attention(qkv, score_bound, diff_lambda_q1[j], diff_lambda_k1[j],
                                    diff_lambda_q2[j], diff_lambda_k2[j], diff_subln[j],
                                    lambda_init)
            w_out = diff_w_out[j]
        xs, xn = _out_project(mixed, w_out.astype(BF16), xs, norm_ffn[layer])
        g_next = norm_mix[layer + 1] if layer + 1 < DEPTH else None
        xs, xn = _ffn(xn, ffn_w_gate[layer].astype(BF16), ffn_w_up[layer].astype(BF16),
                      ffn_w_down[layer].astype(BF16), xs, g_next)
    return xs.reshape(b, s, d)
```

```python
import functools
import math

import jax
import jax.numpy as jnp
import numpy as np
from jax import lax
from jax.experimental import pallas as pl
from jax.experimental.pallas import tpu as pltpu

D_MODEL = 2048
DEPTH = 4
ROPE_THETA = 10000.0
NORM_EPS = 1e-6
RET_HEADS = 8
RET_DK = 64
RET_DV = 128
CONV_WIDTH = 1024
CONV_K = 3
DIFF_HEADS = 8
DIFF_HEAD_DIM = 128
FFN_HIDDEN = 5632
RET_QK = RET_HEADS * RET_DK
RET_V = RET_HEADS * RET_DV
HYB_IN = 2 * RET_QK + 2 * RET_V + 3 * CONV_WIDTH
DIFF_QKV = 6144
DIFF_QK_COLS = 2 * DIFF_HEADS * DIFF_HEAD_DIM

LANES = 128
SUBLANES = 8
VMEM_LIMIT_BYTES = 56 * 1024 * 1024

NORM_TM = 512
PROJ_TM = 1024
PROJ_TN = 512
OUT_TM = 512
FFN_TM = 512
FFN_TF = 512
RET_TS = 256
ATT_TQ = 1024
ATT_TK = 512
MASK_VALUE = -1e30
LOG2_E = math.log2(math.e)
ATT_BOUND_SLACK = 1.02
ATT_BOUND_MAX = 60.0

F32 = jnp.float32
BF16 = jnp.bfloat16


def _compiler_params(semantics):
    return pltpu.CompilerParams(dimension_semantics=semantics,
                                vmem_limit_bytes=VMEM_LIMIT_BYTES)


def _rms_rows(v):
    return v * lax.rsqrt(jnp.mean(v * v, axis=-1, keepdims=True) + NORM_EPS)


def _norm_kernel(x_ref, g_ref, o_ref):
    o_ref[...] = (_rms_rows(x_ref[...]) * g_ref[...]).astype(o_ref.dtype)


def _rmsnorm(x, g):
    s, d = x.shape
    return pl.pallas_call(
        _norm_kernel,
        grid=(s // NORM_TM,),
        in_specs=[pl.BlockSpec((NORM_TM, d), lambda i: (i, 0)),
                  pl.BlockSpec((1, d), lambda i: (0, 0))],
        out_specs=pl.BlockSpec((NORM_TM, d), lambda i: (i, 0)),
        out_shape=jax.ShapeDtypeStruct((s, d), BF16),
        compiler_params=_compiler_params(("parallel",)),
        name="rmsnorm",
    )(x, g.reshape(1, d))


def _rotate_half(blk, half):
    if 2 * half == LANES:
        return pltpu.roll(blk, half, axis=1)
    lane = lax.broadcasted_iota(jnp.int32, blk.shape, 1)
    first = (lane % (2 * half)) < half
    return jnp.where(first, pltpu.roll(blk, LANES - half, axis=1),
                     pltpu.roll(blk, half, axis=1))


def _proj_plain_kernel(x_ref, w_ref, o_ref):
    o_ref[...] = jnp.dot(x_ref[...], w_ref[...],
                         preferred_element_type=F32).astype(o_ref.dtype)


def _proj_rope_kernel(x_ref, w_ref, cos_ref, sin_ref, *rest, rope_half):
    gain_ref, o_ref, acc_ref = rest if len(rest) == 3 else (None,) + rest
    acc_ref[...] = jnp.dot(x_ref[...], w_ref[...], preferred_element_type=F32)

    def lane_group(j, carry):
        cols = pl.ds(pl.multiple_of(j * LANES, LANES), LANES)
        blk = acc_ref[:, cols]
        if gain_ref is not None:
            blk = _rms_rows(blk) * gain_ref[:, cols]
        out = blk * cos_ref[...] + _rotate_half(blk, rope_half) * sin_ref[...]
        o_ref[:, cols] = out.astype(o_ref.dtype)
        return carry

    lax.fori_loop(0, acc_ref.shape[1] // LANES, lane_group, 0)


def _project(xn, w, col_start, n_cols, rope=None):
    s, d = xn.shape
    first = col_start // PROJ_TN
    in_specs = [pl.BlockSpec((PROJ_TM, d), lambda m, j: (m, 0)),
                pl.BlockSpec((d, PROJ_TN), lambda m, j: (0, first + j))]
    args = [xn, w]
    scratch = []
    if rope is None:
        kernel, name = _proj_plain_kernel, "proj_plain"
    else:
        scratch = [pltpu.VMEM((PROJ_TM, PROJ_TN), F32)]
        cos, sin, rope_half, gain = rope
        kernel = functools.partial(_proj_rope_kernel, rope_half=rope_half)
        name = "proj_rope"
        in_specs += [pl.BlockSpec((PROJ_TM, LANES), lambda m, j: (m, 0)),
                     pl.BlockSpec((PROJ_TM, LANES), lambda m, j: (m, 0))]
        args += [cos, sin]
        if gain is not None:
            name = "proj_qknorm_rope"
            in_specs.append(pl.BlockSpec((1, PROJ_TN), lambda m, j: (0, j)))
            args.append(gain)
    return pl.pallas_call(
        kernel,
        grid=(s // PROJ_TM, n_cols // PROJ_TN),
        in_specs=in_specs,
        out_specs=pl.BlockSpec((PROJ_TM, PROJ_TN), lambda m, j: (m, j)),
        out_shape=jax.ShapeDtypeStruct((s, n_cols), BF16),
        scratch_shapes=scratch,
        compiler_params=_compiler_params(("parallel", "arbitrary")),
        name=name,
    )(*args)


def _retention_tables(ts):
    h = np.arange(RET_HEADS, dtype=np.float64)
    log_g = np.log(1.0 - np.exp2(-5.0 - h))
    idx = np.arange(ts, dtype=np.float64)
    dist = idx[:, None] - idx[None, :]
    scale = RET_DK ** -0.5
    decay = np.where(dist >= 0, np.exp(log_g[:, None, None] * np.maximum(dist, 0.0)), 0.0) * scale
    q_dec = np.exp(log_g[:, None] * (idx + 1.0))
    k_dec = np.exp(log_g[:, None] * (ts - 1.0 - idx)) * scale
    chunk_dec = np.exp(log_g * ts)
    q_dec = np.broadcast_to(q_dec[:, :, None], (RET_HEADS, ts, LANES))
    k_dec = np.broadcast_to(k_dec[:, :, None], (RET_HEADS, ts, LANES))
    return (jnp.asarray(decay, F32), jnp.asarray(q_dec, F32), jnp.asarray(k_dec, F32),
            [float(np.float32(c)) for c in chunk_dec])


def _silu(v):
    return v / (1.0 + jnp.exp(-v))


def _hybrid_kernel(q_ref, k_ref, v_ref, g_ref, cb_ref, cc_ref, cx_ref, convw_ref,
                   decay_ref, qdec_ref, kdec_ref, o_ref, state_ref, u_ref, *, chunk_dec):
    ts = q_ref.shape[0]

    @pl.when(pl.program_id(0) == 0)
    def _():
        state_ref[...] = jnp.zeros_like(state_ref)
        u_ref[0:SUBLANES, :] = jnp.zeros((SUBLANES, u_ref.shape[1]), F32)

    lane = lax.broadcasted_iota(jnp.int32, (ts, LANES), 1)
    for h in range(RET_HEADS):
        pair = slice((h // 2) * LANES, (h // 2 + 1) * LANES)
        head = slice(h * RET_DV, (h + 1) * RET_DV)
        in_head = (lane // RET_DK) == (h % 2)
        qm = jnp.where(in_head, q_ref[:, pair].astype(F32), 0.0)
        kp = k_ref[:, pair]
        vh = v_ref[:, head]
        s = lax.dot_general(qm.astype(BF16), kp, (((1,), (1,)), ((), ())),
                            preferred_element_type=F32)
        inner = (s * decay_ref[h]).astype(BF16)
        qd = (qm * qdec_ref[h]).astype(BF16)
        o = (jnp.dot(inner, vh, preferred_element_type=F32)
             + jnp.dot(qd, state_ref[h].astype(BF16), preferred_element_type=F32))
        kd = (kp.astype(F32) * kdec_ref[h]).astype(BF16)
        state_ref[h] = chunk_dec[h] * state_ref[h] + lax.dot_general(
            kd, vh, (((0,), (0,)), ((), ())), preferred_element_type=F32)
        gate = _silu(g_ref[:, head].astype(F32))
        o_ref[:, head] = (gate * _rms_rows(o)).astype(o_ref.dtype)

    u_ref[SUBLANES:SUBLANES + ts, :] = cc_ref[...].astype(F32) * cx_ref[...].astype(F32)
    y = (convw_ref[0:1, :] * u_ref[SUBLANES - 2:SUBLANES - 2 + ts, :]
         + convw_ref[1:2, :] * u_ref[SUBLANES - 1:SUBLANES - 1 + ts, :]
         + convw_ref[2:3, :] * u_ref[SUBLANES:SUBLANES + ts, :])
    o_ref[:, RET_V:] = (cb_ref[...].astype(F32) * y).astype(o_ref.dtype)
    u_ref[0:SUBLANES, :] = u_ref[ts:ts + SUBLANES, :]


def _hybrid_mix(qk, rest, conv_w):
    s = qk.shape[0]
    ts = RET_TS
    decay, q_dec, k_dec, chunk_dec = _retention_tables(ts)
    qk_blk = RET_QK
    w_blk = RET_V
    const3 = lambda i: (0, 0, 0)
    kernel = functools.partial(_hybrid_kernel, chunk_dec=chunk_dec)
    return pl.pallas_call(
        kernel,
        grid=(s // ts,),
        in_specs=[pl.BlockSpec((ts, qk_blk), lambda i: (i, 0)),
                  pl.BlockSpec((ts, qk_blk), lambda i: (i, 1)),
                  pl.BlockSpec((ts, w_blk), lambda i: (i, 0)),
                  pl.BlockSpec((ts, w_blk), lambda i: (i, 1)),
                  pl.BlockSpec((ts, w_blk), lambda i: (i, 2)),
                  pl.BlockSpec((ts, w_blk), lambda i: (i, 3)),
                  pl.BlockSpec((ts, w_blk), lambda i: (i, 4)),
                  pl.BlockSpec((SUBLANES, CONV_WIDTH), lambda i: (0, 0)),
                  pl.BlockSpec((RET_HEADS, ts, ts), const3),
                  pl.BlockSpec((RET_HEADS, ts, LANES), const3),
                  pl.BlockSpec((RET_HEADS, ts, LANES), const3)],
        out_specs=pl.BlockSpec((ts, RET_V + CONV_WIDTH), lambda i: (i, 0)),
        out_shape=jax.ShapeDtypeStruct((s, RET_V + CONV_WIDTH), BF16),
        scratch_shapes=[pltpu.VMEM((RET_HEADS, LANES, RET_DV), F32),
                        pltpu.VMEM((SUBLANES + ts, CONV_WIDTH), F32)],
        compiler_params=_compiler_params(("arbitrary",)),
        name="retention_conv",
    )(qk, qk, rest, rest, rest, rest, rest,
      jnp.pad(conv_w, ((0, SUBLANES - CONV_K), (0, 0))), decay, q_dec, k_dec)


def _lane_partial_sum(p):
    total = p[:, 0:LANES]
    for t in range(1, p.shape[1] // LANES):
        total = total + p[:, t * LANES:(t + 1) * LANES]
    return total


def _attn_kernel(bound_ref, q_ref, k_ref, v_ref, lq1_ref, lk1_ref, lq2_ref, lk2_ref,
                 subln_ref, o_ref, m_ref, l_ref, acc_ref, *, lambda_init):
    tq = q_ref.shape[0]
    tk = ATT_TK
    d = DIFF_HEAD_DIM
    qi = pl.program_id(1)
    bound = bound_ref[0]

    l_ref[...] = jnp.zeros_like(l_ref)
    acc_ref[...] = jnp.zeros_like(acc_ref)

    def scores(j, i, rows, masked):
        start = pl.multiple_of(j * tk, tk)
        s = lax.dot_general(q_ref[rows, i * d:(i + 1) * d],
                            k_ref[pl.ds(start, tk), i * d:(i + 1) * d],
                            (((1,), (1,)), ((), ())), preferred_element_type=F32)
        if masked:
            row = lax.broadcasted_iota(jnp.int32, s.shape, 0)
            col = lax.broadcasted_iota(jnp.int32, s.shape, 1)
            s = jnp.where(col <= row, s, MASK_VALUE)
        return s, v_ref[pl.ds(start, tk), :]

    def update_bounded(j, rows, masked):
        for i in range(2):
            s, v = scores(j, i, rows, masked)
            p = jnp.exp2(s - bound)
            l_ref[i, rows, :] += _lane_partial_sum(p)
            acc_ref[i, rows, :] += jnp.dot(p.astype(BF16), v, preferred_element_type=F32)

    def update_online(j, rows, masked):
        for i in range(2):
            s, v = scores(j, i, rows, masked)
            m_prev = m_ref[i, rows, :]
            m_new = jnp.maximum(m_prev, jnp.max(s, axis=1, keepdims=True))
            alpha = jnp.exp2(m_prev - m_new)
            p = jnp.exp2(s - m_new[:, :1])
            l_ref[i, rows, :] = alpha * l_ref[i, rows, :] + _lane_partial_sum(p)
            acc_ref[i, rows, :] = alpha[:, :1] * acc_ref[i, rows, :] + jnp.dot(
                p.astype(BF16), v, preferred_element_type=F32)
            m_ref[i, rows, :] = m_new

    def run(update):
        blocks_per_step = tq // tk
        all_rows = slice(0, tq)

        def body(j, carry):
            update(j, all_rows, masked=False)
            return carry
        lax.fori_loop(0, blocks_per_step * qi, body, 0)
        for r in range(blocks_per_step):
            update(blocks_per_step * qi + r, slice(r * tk, tq), masked=True)

    @pl.when(bound <= ATT_BOUND_MAX)
    def _():
        run(update_bounded)

    @pl.when(bound > ATT_BOUND_MAX)
    def _():
        m_ref[...] = jnp.full_like(m_ref, MASK_VALUE)
        run(update_online)

    lam = (jnp.exp(jnp.sum(lq1_ref[...] * lk1_ref[...], axis=1, keepdims=True))
           - jnp.exp(jnp.sum(lq2_ref[...] * lk2_ref[...], axis=1, keepdims=True))
           + lambda_init)
    l0 = jnp.sum(l_ref[0], axis=1, keepdims=True)
    l1 = jnp.sum(l_ref[1], axis=1, keepdims=True)
    o = acc_ref[0] / l0 - lam * (acc_ref[1] / l1)
    o = _rms_rows(o) * subln_ref[...] * (1.0 - lambda_init)
    o_ref[...] = o.astype(o_ref.dtype)


def _diff_attention(qk, v, score_bound, lq1, lk1, lq2, lk2, subln, lambda_init):
    s = qk.shape[0]
    hd = 2 * DIFF_HEAD_DIM
    assert ATT_TQ % ATT_TK == 0
    vec = lambda a: a.reshape(1, -1).astype(F32)
    small = lambda w: pl.BlockSpec((1, w), lambda h, i: (0, 0))
    kernel = functools.partial(_attn_kernel, lambda_init=lambda_init)
    return pl.pallas_call(
        kernel,
        grid=(DIFF_HEADS, s // ATT_TQ),
        in_specs=[pl.BlockSpec(memory_space=pltpu.SMEM),
                  pl.BlockSpec((ATT_TQ, hd), lambda h, i: (i, h)),
                  pl.BlockSpec((s, hd), lambda h, i: (0, DIFF_HEADS + h)),
                  pl.BlockSpec((s, hd), lambda h, i: (0, h)),
                  small(DIFF_HEAD_DIM), small(DIFF_HEAD_DIM),
                  small(DIFF_HEAD_DIM), small(DIFF_HEAD_DIM), small(hd)],
        out_specs=pl.BlockSpec((ATT_TQ, hd), lambda h, i: (i, h)),
        out_shape=jax.ShapeDtypeStruct((s, DIFF_HEADS * hd), BF16),
        scratch_shapes=[pltpu.VMEM((2, ATT_TQ, LANES), F32),
                        pltpu.VMEM((2, ATT_TQ, LANES), F32),
                        pltpu.VMEM((2, ATT_TQ, hd), F32)],
        compiler_params=_compiler_params(("parallel", "arbitrary")),
        name="diff_attention",
    )(score_bound, qk, qk, v, vec(lq1), vec(lk1), vec(lq2), vec(lk2), vec(subln))


def _out_kernel(a_ref, w_ref, x_ref, g_ref, xo_ref, xn_ref):
    x = x_ref[...] + jnp.dot(a_ref[...], w_ref[...], preferred_element_type=F32)
    xo_ref[...] = x
    xn_ref[...] = (_rms_rows(x) * g_ref[...]).astype(xn_ref.dtype)


def _out_project(a, w, x, g):
    s, d = x.shape
    row = lambda i: (i, 0)
    return pl.pallas_call(
        _out_kernel,
        grid=(s // OUT_TM,),
        in_specs=[pl.BlockSpec((OUT_TM, a.shape[1]), row),
                  pl.BlockSpec(w.shape, lambda i: (0, 0)),
                  pl.BlockSpec((OUT_TM, d), row),
                  pl.BlockSpec((1, d), lambda i: (0, 0))],
        out_specs=[pl.BlockSpec((OUT_TM, d), row), pl.BlockSpec((OUT_TM, d), row)],
        out_shape=[jax.ShapeDtypeStruct((s, d), F32), jax.ShapeDtypeStruct((s, d), BF16)],
        compiler_params=_compiler_params(("parallel",)),
        name="out_proj_residual_norm",
    )(a, w, x, g.reshape(1, d))


def _ffn_kernel(xn_ref, wg_ref, wu_ref, wd_ref, x_ref, *rest):
    g_ref, xo_ref, xn_out_ref = rest if len(rest) == 3 else (None, rest[0], None)
    f = pl.program_id(1)

    @pl.when(f == 0)
    def _():
        xo_ref[...] = x_ref[...]

    xn = xn_ref[...]
    gate = jnp.dot(xn, wg_ref[...], preferred_element_type=F32)
    up = jnp.dot(xn, wu_ref[...], preferred_element_type=F32)
    act = (_silu(gate) * up).astype(BF16)
    xo_ref[...] += jnp.dot(act, wd_ref[...], preferred_element_type=F32)

    if xn_out_ref is not None:
        @pl.when(f == pl.num_programs(1) - 1)
        def _():
            xn_out_ref[...] = (_rms_rows(xo_ref[...]) * g_ref[...]).astype(xn_out_ref.dtype)


def _ffn(xn, wg, wu, wd, x, g):
    s, d = x.shape
    hidden = wg.shape[1]
    row = lambda i, f: (i, 0)
    in_specs = [pl.BlockSpec((FFN_TM, d), row),
                pl.BlockSpec((d, FFN_TF), lambda i, f: (0, f)),
                pl.BlockSpec((d, FFN_TF), lambda i, f: (0, f)),
                pl.BlockSpec((FFN_TF, d), lambda i, f: (f, 0)),
                pl.BlockSpec((FFN_TM, d), row)]
    out_specs = [pl.BlockSpec((FFN_TM, d), row)]
    out_shape = [jax.ShapeDtypeStruct((s, d), F32)]
    args = [xn, wg, wu, wd, x]
    if g is not None:
        in_specs.append(pl.BlockSpec((1, d), lambda i, f: (0, 0)))
        out_specs.append(pl.BlockSpec((FFN_TM, d), row))
        out_shape.append(jax.ShapeDtypeStruct((s, d), BF16))
        args.append(g.reshape(1, d))
    outs = pl.pallas_call(
        _ffn_kernel,
        grid=(s // FFN_TM, hidden // FFN_TF),
        in_specs=in_specs,
        out_specs=out_specs,
        out_shape=out_shape,
        compiler_params=_compiler_params(("parallel", "arbitrary")),
        name="swiglu_ffn_residual_norm" if g is not None else "swiglu_ffn_residual",
    )(*args)
    return (outs[0], outs[1]) if g is not None else (outs[0], None)


def _rope_tables(seq, dim):
    inv = ROPE_THETA ** (-jnp.arange(0, dim, 2, dtype=F32) / dim)
    ang = jnp.arange(seq, dtype=F32)[:, None] * inv[None, :]
    cos = jnp.cos(ang)
    sin = jnp.sin(ang)
    reps = LANES // dim
    cos_l = jnp.tile(jnp.concatenate([cos, cos], axis=1), (1, reps))
    sin_l = jnp.tile(jnp.concatenate([-sin, sin], axis=1), (1, reps))
    return cos_l, sin_l


def kernel(x, norm_mix, norm_ffn, hyb_w_in, hyb_conv_w, hyb_w_out, diff_w_qkv, diff_q_norm,
           diff_k_norm, diff_lambda_q1, diff_lambda_k1, diff_lambda_q2, diff_lambda_k2,
           diff_subln, diff_w_out, ffn_w_gate, ffn_w_up, ffn_w_down):
    b, s, d = x.shape
    assert b == 1 and d == D_MODEL
    xs = x.reshape(s, d)
    cos_r, sin_r = _rope_tables(s, RET_DK)
    cos_a, sin_a = _rope_tables(s, DIFF_HEAD_DIM)

    xn = _rmsnorm(xs, norm_mix[0])
    for layer in range(DEPTH):
        j = layer // 2
        if layer % 2 == 0:
            w_in = hyb_w_in[j].astype(BF16)
            qk_cols = 2 * RET_QK
            qk = _project(xn, w_in, 0, qk_cols, rope=(cos_r, sin_r, RET_DK // 2, None))
            rest = _project(xn, w_in, qk_cols, HYB_IN - qk_cols)
            mixed = _hybrid_mix(qk, rest, hyb_conv_w[j])
            w_out = hyb_w_out[j]
        else:
            lambda_init = 0.8 - 0.6 * math.exp(-0.3 * layer)
            n_groups = DIFF_QK_COLS // DIFF_HEAD_DIM
            q_gain = diff_q_norm[j] * (LOG2_E * DIFF_HEAD_DIM ** -0.5)
            gain = jnp.concatenate([jnp.tile(q_gain, n_groups),
                                    jnp.tile(diff_k_norm[j], n_groups)]).reshape(1, -1)
            w_qkv = diff_w_qkv[j].astype(BF16)
            qk_cols = 2 * DIFF_QK_COLS
            qk = _project(xn, w_qkv, 0, qk_cols,
                          rope=(cos_a, sin_a, DIFF_HEAD_DIM // 2, gain))
            v = _project(xn, w_qkv, qk_cols, DIFF_QKV - qk_cols)
            score_bound = (ATT_BOUND_SLACK * DIFF_HEAD_DIM * jnp.max(jnp.abs(q_gain))
                           * jnp.max(jnp.abs(diff_k_norm[j]))).reshape(1).astype(F32)
            mixed = _diff_attention(qk, v, score_bound, diff_lambda_q1[j], diff_lambda_k1[j],
                                    diff_lambda_q2[j], diff_lambda_k2[j], diff_subln[j],
                                    lambda_init)
            w_out = diff_w_out[j]
        xs, xn = _out_project(mixed, w_out.astype(BF16), xs, norm_ffn[layer])
        g_next = norm_mix[layer + 1] if layer + 1 < DEPTH else None
        xs, xn = _ffn(xn, ffn_w_gate[layer].astype(BF16), ffn_w_up[layer].astype(BF16),
                      ffn_w_down[layer].astype(BF16), xs, g_next)
    return xs.reshape(b, s, d)
```

```python
import functools
import math

import jax
import jax.numpy as jnp
import numpy as np
from jax import lax
from jax.experimental import pallas as pl
from jax.experimental.pallas import tpu as pltpu

D_MODEL = 2048
DEPTH = 4
ROPE_THETA = 10000.0
NORM_EPS = 1e-6
RET_HEADS = 8
RET_DK = 64
RET_DV = 128
CONV_WIDTH = 1024
CONV_K = 3
DIFF_HEADS = 8
DIFF_HEAD_DIM = 128
FFN_HIDDEN = 5632
RET_QK = RET_HEADS * RET_DK
RET_V = RET_HEADS * RET_DV
HYB_IN = 2 * RET_QK + 2 * RET_V + 3 * CONV_WIDTH
DIFF_QKV = 6144
DIFF_QK_COLS = 2 * DIFF_HEADS * DIFF_HEAD_DIM

LANES = 128
SUBLANES = 8
VMEM_LIMIT_BYTES = 56 * 1024 * 1024

NORM_TM = 512
PROJ_TM = 1024
PROJ_TN = 512
OUT_TM = 512
FFN_TM = 1024
FFN_TF = 256
RET_TS = 256
ATT_TQ = 1024
ATT_TK = 512
MASK_VALUE = -1e30
LOG2_E = math.log2(math.e)
ATT_BOUND_SLACK = 1.02
ATT_BOUND_MAX = 60.0

F32 = jnp.float32
BF16 = jnp.bfloat16


def _compiler_params(semantics):
    return pltpu.CompilerParams(dimension_semantics=semantics,
                                vmem_limit_bytes=VMEM_LIMIT_BYTES)


def _rms_rows(v):
    return v * lax.rsqrt(jnp.mean(v * v, axis=-1, keepdims=True) + NORM_EPS)


def _norm_kernel(x_ref, g_ref, o_ref):
    o_ref[...] = (_rms_rows(x_ref[...]) * g_ref[...]).astype(o_ref.dtype)


def _rmsnorm(x, g):
    s, d = x.shape
    return pl.pallas_call(
        _norm_kernel,
        grid=(s // NORM_TM,),
        in_specs=[pl.BlockSpec((NORM_TM, d), lambda i: (i, 0)),
                  pl.BlockSpec((1, d), lambda i: (0, 0))],
        out_specs=pl.BlockSpec((NORM_TM, d), lambda i: (i, 0)),
        out_shape=jax.ShapeDtypeStruct((s, d), BF16),
        compiler_params=_compiler_params(("parallel",)),
        name="rmsnorm",
    )(x, g.reshape(1, d))


def _rotate_half(blk, half):
    if 2 * half == LANES:
        return pltpu.roll(blk, half, axis=1)
    lane = lax.broadcasted_iota(jnp.int32, blk.shape, 1)
    first = (lane % (2 * half)) < half
    return jnp.where(first, pltpu.roll(blk, LANES - half, axis=1),
                     pltpu.roll(blk, half, axis=1))


def _proj_plain_kernel(x_ref, w_ref, o_ref):
    o_ref[...] = jnp.dot(x_ref[...], w_ref[...].astype(BF16),
                         preferred_element_type=F32).astype(o_ref.dtype)


def _proj_rope_kernel(x_ref, w_ref, cos_ref, sin_ref, *rest, rope_half):
    gain_ref, o_ref, acc_ref = rest if len(rest) == 3 else (None,) + rest
    acc_ref[...] = jnp.dot(x_ref[...], w_ref[...].astype(BF16), preferred_element_type=F32)

    def lane_group(j, carry):
        cols = pl.ds(pl.multiple_of(j * LANES, LANES), LANES)
        blk = acc_ref[:, cols]
        if gain_ref is not None:
            blk = _rms_rows(blk) * gain_ref[:, cols]
        out = blk * cos_ref[...] + _rotate_half(blk, rope_half) * sin_ref[...]
        o_ref[:, cols] = out.astype(o_ref.dtype)
        return carry

    lax.fori_loop(0, acc_ref.shape[1] // LANES, lane_group, 0)


def _project(xn, w, layer, col_start, n_cols, rope=None):
    s, d = xn.shape
    first = col_start // PROJ_TN
    in_specs = [pl.BlockSpec((PROJ_TM, d), lambda m, j: (m, 0)),
                pl.BlockSpec((None, d, PROJ_TN), lambda m, j: (layer, 0, first + j))]
    args = [xn, w]
    scratch = []
    if rope is None:
        kernel, name = _proj_plain_kernel, "proj_plain"
    else:
        scratch = [pltpu.VMEM((PROJ_TM, PROJ_TN), F32)]
        cos, sin, rope_half, gain = rope
        kernel = functools.partial(_proj_rope_kernel, rope_half=rope_half)
        name = "proj_rope"
        in_specs += [pl.BlockSpec((PROJ_TM, LANES), lambda m, j: (m, 0)),
                     pl.BlockSpec((PROJ_TM, LANES), lambda m, j: (m, 0))]
        args += [cos, sin]
        if gain is not None:
            name = "proj_qknorm_rope"
            in_specs.append(pl.BlockSpec((1, PROJ_TN), lambda m, j: (0, j)))
            args.append(gain)
    return pl.pallas_call(
        kernel,
        grid=(s // PROJ_TM, n_cols // PROJ_TN),
        in_specs=in_specs,
        out_specs=pl.BlockSpec((PROJ_TM, PROJ_TN), lambda m, j: (m, j)),
        out_shape=jax.ShapeDtypeStruct((s, n_cols), BF16),
        scratch_shapes=scratch,
        compiler_params=_compiler_params(("parallel", "arbitrary")),
        name=name,
    )(*args)


def _retention_tables(ts):
    h = np.arange(RET_HEADS, dtype=np.float64)
    log_g = np.log(1.0 - np.exp2(-5.0 - h))
    idx = np.arange(ts, dtype=np.float64)
    dist = idx[:, None] - idx[None, :]
    scale = RET_DK ** -0.5
    decay = np.where(dist >= 0, np.exp(log_g[:, None, None] * np.maximum(dist, 0.0)), 0.0) * scale
    q_dec = np.exp(log_g[:, None] * (idx + 1.0))
    k_dec = np.exp(log_g[:, None] * (ts - 1.0 - idx)) * scale
    chunk_dec = np.exp(log_g * ts)
    q_dec = np.broadcast_to(q_dec[:, :, None], (RET_HEADS, ts, LANES))
    k_dec = np.broadcast_to(k_dec[:, :, None], (RET_HEADS, ts, LANES))
    return (jnp.asarray(decay, F32), jnp.asarray(q_dec, F32), jnp.asarray(k_dec, F32),
            [float(np.float32(c)) for c in chunk_dec])


def _silu(v):
    return v / (1.0 + jnp.exp(-v))


def _hybrid_kernel(q_ref, k_ref, v_ref, g_ref, cb_ref, cc_ref, cx_ref, convw_ref,
                   decay_ref, qdec_ref, kdec_ref, o_ref, state_ref, u_ref, *, chunk_dec):
    ts = q_ref.shape[0]

    @pl.when(pl.program_id(0) == 0)
    def _():
        state_ref[...] = jnp.zeros_like(state_ref)
        u_ref[0:SUBLANES, :] = jnp.zeros((SUBLANES, u_ref.shape[1]), F32)

    lane = lax.broadcasted_iota(jnp.int32, (ts, LANES), 1)
    for h in range(RET_HEADS):
        pair = slice((h // 2) * LANES, (h // 2 + 1) * LANES)
        head = slice(h * RET_DV, (h + 1) * RET_DV)
        in_head = (lane // RET_DK) == (h % 2)
        qm = jnp.where(in_head, q_ref[:, pair].astype(F32), 0.0)
        kp = k_ref[:, pair]
        vh = v_ref[:, head]
        s = lax.dot_general(qm.astype(BF16), kp, (((1,), (1,)), ((), ())),
                            preferred_element_type=F32)
        inner = (s * decay_ref[h]).astype(BF16)
        qd = (qm * qdec_ref[h]).astype(BF16)
        o = (jnp.dot(inner, vh, preferred_element_type=F32)
             + jnp.dot(qd, state_ref[h].astype(BF16), preferred_element_type=F32))
        kd = (kp.astype(F32) * kdec_ref[h]).astype(BF16)
        state_ref[h] = chunk_dec[h] * state_ref[h] + lax.dot_general(
            kd, vh, (((0,), (0,)), ((), ())), preferred_element_type=F32)
        gate = _silu(g_ref[:, head].astype(F32))
        o_ref[:, head] = (gate * _rms_rows(o)).astype(o_ref.dtype)

    u_ref[SUBLANES:SUBLANES + ts, :] = cc_ref[...].astype(F32) * cx_ref[...].astype(F32)
    y = (convw_ref[0:1, :] * u_ref[SUBLANES - 2:SUBLANES - 2 + ts, :]
         + convw_ref[1:2, :] * u_ref[SUBLANES - 1:SUBLANES - 1 + ts, :]
         + convw_ref[2:3, :] * u_ref[SUBLANES:SUBLANES + ts, :])
    o_ref[:, RET_V:] = (cb_ref[...].astype(F32) * y).astype(o_ref.dtype)
    u_ref[0:SUBLANES, :] = u_ref[ts:ts + SUBLANES, :]


def _hybrid_mix(qk, rest, conv_w):
    s = qk.shape[0]
    ts = RET_TS
    decay, q_dec, k_dec, chunk_dec = _retention_tables(ts)
    qk_blk = RET_QK
    w_blk = RET_V
    const3 = lambda i: (0, 0, 0)
    kernel = functools.partial(_hybrid_kernel, chunk_dec=chunk_dec)
    return pl.pallas_call(
        kernel,
        grid=(s // ts,),
        in_specs=[pl.BlockSpec((ts, qk_blk), lambda i: (i, 0)),
                  pl.BlockSpec((ts, qk_blk), lambda i: (i, 1)),
                  pl.BlockSpec((ts, w_blk), lambda i: (i, 0)),
                  pl.BlockSpec((ts, w_blk), lambda i: (i, 1)),
                  pl.BlockSpec((ts, w_blk), lambda i: (i, 2)),
                  pl.BlockSpec((ts, w_blk), lambda i: (i, 3)),
                  pl.BlockSpec((ts, w_blk), lambda i: (i, 4)),
                  pl.BlockSpec((SUBLANES, CONV_WIDTH), lambda i: (0, 0)),
                  pl.BlockSpec((RET_HEADS, ts, ts), const3),
                  pl.BlockSpec((RET_HEADS, ts, LANES), const3),
                  pl.BlockSpec((RET_HEADS, ts, LANES), const3)],
        out_specs=pl.BlockSpec((ts, RET_V + CONV_WIDTH), lambda i: (i, 0)),
        out_shape=jax.ShapeDtypeStruct((s, RET_V + CONV_WIDTH), BF16),
        scratch_shapes=[pltpu.VMEM((RET_HEADS, LANES, RET_DV), F32),
                        pltpu.VMEM((SUBLANES + ts, CONV_WIDTH), F32)],
        compiler_params=_compiler_params(("arbitrary",)),
        name="retention_conv",
    )(qk, qk, rest, rest, rest, rest, rest,
      jnp.pad(conv_w, ((0, SUBLANES - CONV_K), (0, 0))), decay, q_dec, k_dec)


def _lane_partial_sum(p):
    total = p[:, 0:LANES]
    for t in range(1, p.shape[1] // LANES):
        total = total + p[:, t * LANES:(t + 1) * LANES]
    return total


def _attn_kernel(bound_ref, q_ref, k_ref, v_ref, lq1_ref, lk1_ref, lq2_ref, lk2_ref,
                 subln_ref, o_ref, m_ref, l_ref, acc_ref, *, lambda_init):
    tq = q_ref.shape[0]
    tk = ATT_TK
    d = DIFF_HEAD_DIM
    qi = pl.program_id(1)
    bound = bound_ref[0]

    l_ref[...] = jnp.zeros_like(l_ref)
    acc_ref[...] = jnp.zeros_like(acc_ref)

    def scores(j, i, rows, masked):
        start = pl.multiple_of(j * tk, tk)
        s = lax.dot_general(q_ref[rows, i * d:(i + 1) * d],
                            k_ref[pl.ds(start, tk), i * d:(i + 1) * d],
                            (((1,), (1,)), ((), ())), preferred_element_type=F32)
        if masked:
            row = lax.broadcasted_iota(jnp.int32, s.shape, 0)
            col = lax.broadcasted_iota(jnp.int32, s.shape, 1)
            s = jnp.where(col <= row, s, MASK_VALUE)
        return s, v_ref[pl.ds(start, tk), :]

    def update_bounded(j, rows, masked):
        for i in range(2):
            s, v = scores(j, i, rows, masked)
            p = jnp.exp2(s - bound)
            l_ref[i, rows, :] += _lane_partial_sum(p)
            acc_ref[i, rows, :] += jnp.dot(p.astype(BF16), v, preferred_element_type=F32)

    def update_online(j, rows, masked):
        for i in range(2):
            s, v = scores(j, i, rows, masked)
            m_prev = m_ref[i, rows, :]
            m_new = jnp.maximum(m_prev, jnp.max(s, axis=1, keepdims=True))
            alpha = jnp.exp2(m_prev - m_new)
            p = jnp.exp2(s - m_new[:, :1])
            l_ref[i, rows, :] = alpha * l_ref[i, rows, :] + _lane_partial_sum(p)
            acc_ref[i, rows, :] = alpha[:, :1] * acc_ref[i, rows, :] + jnp.dot(
                p.astype(BF16), v, preferred_element_type=F32)
            m_ref[i, rows, :] = m_new

    def run(update):
        blocks_per_step = tq // tk
        all_rows = slice(0, tq)

        def body(j, carry):
            update(j, all_rows, masked=False)
            return carry
        lax.fori_loop(0, blocks_per_step * qi, body, 0)
        for r in range(blocks_per_step):
            update(blocks_per_step * qi + r, slice(r * tk, tq), masked=True)

    @pl.when(bound <= ATT_BOUND_MAX)
    def _():
        run(update_bounded)

    @pl.when(bound > ATT_BOUND_MAX)
    def _():
        m_ref[...] = jnp.full_like(m_ref, MASK_VALUE)
        run(update_online)

    lam = (jnp.exp(jnp.sum(lq1_ref[...] * lk1_ref[...], axis=1, keepdims=True))
           - jnp.exp(jnp.sum(lq2_ref[...] * lk2_ref[...], axis=1, keepdims=True))
           + lambda_init)
    l0 = jnp.sum(l_ref[0], axis=1, keepdims=True)
    l1 = jnp.sum(l_ref[1], axis=1, keepdims=True)
    o = acc_ref[0] / l0 - lam * (acc_ref[1] / l1)
    o = _rms_rows(o) * subln_ref[...] * (1.0 - lambda_init)
    o_ref[...] = o.astype(o_ref.dtype)


def _diff_attention(qk, v, score_bound, lq1, lk1, lq2, lk2, subln, lambda_init):
    s = qk.shape[0]
    hd = 2 * DIFF_HEAD_DIM
    assert ATT_TQ % ATT_TK == 0
    vec = lambda a: a.reshape(1, -1).astype(F32)
    small = lambda w: pl.BlockSpec((1, w), lambda h, i: (0, 0))
    kernel = functools.partial(_attn_kernel, lambda_init=lambda_init)
    return pl.pallas_call(
        kernel,
        grid=(DIFF_HEADS, s // ATT_TQ),
        in_specs=[pl.BlockSpec(memory_space=pltpu.SMEM),
                  pl.BlockSpec((ATT_TQ, hd), lambda h, i: (i, h)),
                  pl.BlockSpec((s, hd), lambda h, i: (0, DIFF_HEADS + h)),
                  pl.BlockSpec((s, hd), lambda h, i: (0, h)),
                  small(DIFF_HEAD_DIM), small(DIFF_HEAD_DIM),
                  small(DIFF_HEAD_DIM), small(DIFF_HEAD_DIM), small(hd)],
        out_specs=pl.BlockSpec((ATT_TQ, hd), lambda h, i: (i, h)),
        out_shape=jax.ShapeDtypeStruct((s, DIFF_HEADS * hd), BF16),
        scratch_shapes=[pltpu.VMEM((2, ATT_TQ, LANES), F32),
                        pltpu.VMEM((2, ATT_TQ, LANES), F32),
                        pltpu.VMEM((2, ATT_TQ, hd), F32)],
        compiler_params=_compiler_params(("parallel", "arbitrary")),
        name="diff_attention",
    )(score_bound, qk, qk, v, vec(lq1), vec(lk1), vec(lq2), vec(lk2), vec(subln))


def _out_kernel(a_ref, w_ref, x_ref, g_ref, xo_ref, xn_ref):
    x = x_ref[...] + jnp.dot(a_ref[...], w_ref[...], preferred_element_type=F32)
    xo_ref[...] = x
    xn_ref[...] = (_rms_rows(x) * g_ref[...]).astype(xn_ref.dtype)


def _out_project(a, w, x, g):
    s, d = x.shape
    row = lambda i: (i, 0)
    return pl.pallas_call(
        _out_kernel,
        grid=(s // OUT_TM,),
        in_specs=[pl.BlockSpec((OUT_TM, a.shape[1]), row),
                  pl.BlockSpec(w.shape, lambda i: (0, 0)),
                  pl.BlockSpec((OUT_TM, d), row),
                  pl.BlockSpec((1, d), lambda i: (0, 0))],
        out_specs=[pl.BlockSpec((OUT_TM, d), row), pl.BlockSpec((OUT_TM, d), row)],
        out_shape=[jax.ShapeDtypeStruct((s, d), F32), jax.ShapeDtypeStruct((s, d), BF16)],
        compiler_params=_compiler_params(("parallel",)),
        name="out_proj_residual_norm",
    )(a, w, x, g.reshape(1, d))


def _ffn_kernel(xn_ref, wg_ref, wu_ref, wd_ref, x_ref, *rest):
    g_ref, xo_ref, xn_out_ref = rest if len(rest) == 3 else (None, rest[0], None)
    f = pl.program_id(1)

    @pl.when(f == 0)
    def _():
        xo_ref[...] = x_ref[...]

    xn = xn_ref[...]
    gate = jnp.dot(xn, wg_ref[...].astype(BF16), preferred_element_type=F32)
    up = jnp.dot(xn, wu_ref[...].astype(BF16), preferred_element_type=F32)
    act = (_silu(gate) * up).astype(BF16)
    xo_ref[...] += jnp.dot(act, wd_ref[...].astype(BF16), preferred_element_type=F32)

    if xn_out_ref is not None:
        @pl.when(f == pl.num_programs(1) - 1)
        def _():
            xn_out_ref[...] = (_rms_rows(xo_ref[...]) * g_ref[...]).astype(xn_out_ref.dtype)


def _ffn(xn, wg, wu, wd, layer, x, g):
    s, d = x.shape
    hidden = wg.shape[2]
    row_block = lambda: pl.BlockSpec((FFN_TM, d), lambda i, f: (i, 0),
                                     pipeline_mode=pl.Buffered(1))
    in_specs = [row_block(),
                pl.BlockSpec((None, d, FFN_TF), lambda i, f: (layer, 0, f)),
                pl.BlockSpec((None, d, FFN_TF), lambda i, f: (layer, 0, f)),
                pl.BlockSpec((None, FFN_TF, d), lambda i, f: (layer, f, 0)),
                row_block()]
    out_specs = [row_block()]
    out_shape = [jax.ShapeDtypeStruct((s, d), F32)]
    args = [xn, wg, wu, wd, x]
    if g is not None:
        in_specs.append(pl.BlockSpec((1, d), lambda i, f: (0, 0)))
        out_specs.append(row_block())
        out_shape.append(jax.ShapeDtypeStruct((s, d), BF16))
        args.append(g.reshape(1, d))
    outs = pl.pallas_call(
        _ffn_kernel,
        grid=(s // FFN_TM, hidden // FFN_TF),
        in_specs=in_specs,
        out_specs=out_specs,
        out_shape=out_shape,
        compiler_params=_compiler_params(("parallel", "arbitrary")),
        name="swiglu_ffn_residual_norm" if g is not None else "swiglu_ffn_residual",
    )(*args)
    return (outs[0], outs[1]) if g is not None else (outs[0], None)


def _rope_tables(seq, dim):
    inv = ROPE_THETA ** (-jnp.arange(0, dim, 2, dtype=F32) / dim)
    ang = jnp.arange(seq, dtype=F32)[:, None] * inv[None, :]
    cos = jnp.cos(ang)
    sin = jnp.sin(ang)
    reps = LANES // dim
    cos_l = jnp.tile(jnp.concatenate([cos, cos], axis=1), (1, reps))
    sin_l = jnp.tile(jnp.concatenate([-sin, sin], axis=1), (1, reps))
    return cos_l, sin_l


def kernel(x, norm_mix, norm_ffn, hyb_w_in, hyb_conv_w, hyb_w_out, diff_w_qkv, diff_q_norm,
           diff_k_norm, diff_lambda_q1, diff_lambda_k1, diff_lambda_q2, diff_lambda_k2,
           diff_subln, diff_w_out, ffn_w_gate, ffn_w_up, ffn_w_down):
    b, s, d = x.shape
    assert b == 1 and d == D_MODEL
    xs = x.reshape(s, d)
    cos_r, sin_r = _rope_tables(s, RET_DK)
    cos_a, sin_a = _rope_tables(s, DIFF_HEAD_DIM)

    xn = _rmsnorm(xs, norm_mix[0])
    for layer in range(DEPTH):
        j = layer // 2
        if layer % 2 == 0:
            qk_cols = 2 * RET_QK
            qk = _project(xn, hyb_w_in, j, 0, qk_cols, rope=(cos_r, sin_r, RET_DK // 2, None))
            rest = _project(xn, hyb_w_in, j, qk_cols, HYB_IN - qk_cols)
            mixed = _hybrid_mix(qk, rest, hyb_conv_w[j])
            w_out = hyb_w_out[j]
        else:
            lambda_init = 0.8 - 0.6 * math.exp(-0.3 * layer)
            n_groups = DIFF_QK_COLS // DIFF_HEAD_DIM
            q_gain = diff_q_norm[j] * (LOG2_E * DIFF_HEAD_DIM ** -0.5)
            gain = jnp.concatenate([jnp.tile(q_gain, n_groups),
                                    jnp.tile(diff_k_norm[j], n_groups)]).reshape(1, -1)
            qk_cols = 2 * DIFF_QK_COLS
            qk = _project(xn, diff_w_qkv, j, 0, qk_cols,
                          rope=(cos_a, sin_a, DIFF_HEAD_DIM // 2, gain))
            v = _project(xn, diff_w_qkv, j, qk_cols, DIFF_QKV - qk_cols)
            score_bound = (ATT_BOUND_SLACK * DIFF_HEAD_DIM * jnp.max(jnp.abs(q_gain))
                           * jnp.max(jnp.abs(diff_k_norm[j]))).reshape(1).astype(F32)
            mixed = _diff_attention(qk, v, score_bound, diff_lambda_q1[j], diff_lambda_k1[j],
                                    diff_lambda_q2[j], diff_lambda_k2[j], diff_subln[j],
                                    lambda_init)
            w_out = diff_w_out[j]
        xs, xn = _out_project(mixed, w_out.astype(BF16), xs, norm_ffn[layer])
        g_next = norm_mix[layer + 1] if layer + 1 < DEPTH else None
        xs, xn = _ffn(xn, ffn_w_gate, ffn_w_up, ffn_w_down, layer, xs, g_next)
    return xs.reshape(b, s, d)
```

```python
import functools
import math

import jax
import jax.numpy as jnp
import numpy as np
from jax import lax
from jax.experimental import pallas as pl
from jax.experimental.pallas import tpu as pltpu

D_MODEL = 2048
DEPTH = 4
ROPE_THETA = 10000.0
NORM_EPS = 1e-6
RET_HEADS = 8
RET_DK = 64
RET_DV = 128
CONV_WIDTH = 1024
CONV_K = 3
DIFF_HEADS = 8
DIFF_HEAD_DIM = 128
FFN_HIDDEN = 5632
RET_QK = RET_HEADS * RET_DK
RET_V = RET_HEADS * RET_DV
HYB_IN = 2 * RET_QK + 2 * RET_V + 3 * CONV_WIDTH
DIFF_QKV = 6144
DIFF_QK_COLS = 2 * DIFF_HEADS * DIFF_HEAD_DIM

LANES = 128
SUBLANES = 8
VMEM_LIMIT_BYTES = 56 * 1024 * 1024

NORM_TM = 512
PROJ_TM = 1024
PROJ_TN = 512
OUT_TM = 512
FFN_TM = 1024
FFN_TF = 256
RET_TS = 256
ATT_TQ = 1024
ATT_TK = 512
MASK_VALUE = -1e30
LOG2_E = math.log2(math.e)
ATT_BOUND_SLACK = 1.02
ATT_BOUND_MAX = 60.0

F32 = jnp.float32
BF16 = jnp.bfloat16


def _compiler_params(semantics):
    return pltpu.CompilerParams(dimension_semantics=semantics,
                                vmem_limit_bytes=VMEM_LIMIT_BYTES)


def _rms_rows(v):
    return v * lax.rsqrt(jnp.mean(v * v, axis=-1, keepdims=True) + NORM_EPS)


def _norm_kernel(x_ref, g_ref, o_ref):
    o_ref[...] = (_rms_rows(x_ref[...]) * g_ref[...]).astype(o_ref.dtype)


def _rmsnorm(x, g):
    s, d = x.shape
    return pl.pallas_call(
        _norm_kernel,
        grid=(s // NORM_TM,),
        in_specs=[pl.BlockSpec((NORM_TM, d), lambda i: (i, 0)),
                  pl.BlockSpec((1, d), lambda i: (0, 0))],
        out_specs=pl.BlockSpec((NORM_TM, d), lambda i: (i, 0)),
        out_shape=jax.ShapeDtypeStruct((s, d), BF16),
        compiler_params=_compiler_params(("parallel",)),
        name="rmsnorm",
    )(x, g.reshape(1, d))


def _rotate_half(blk, half):
    if 2 * half == LANES:
        return pltpu.roll(blk, half, axis=1)
    lane = lax.broadcasted_iota(jnp.int32, blk.shape, 1)
    first = (lane % (2 * half)) < half
    return jnp.where(first, pltpu.roll(blk, LANES - half, axis=1),
                     pltpu.roll(blk, half, axis=1))


def _proj_plain_kernel(x_ref, w_ref, o_ref):
    o_ref[...] = jnp.dot(x_ref[...], w_ref[...].astype(BF16),
                         preferred_element_type=F32).astype(o_ref.dtype)


def _proj_rope_kernel(x_ref, w_ref, cos_ref, sin_ref, *rest, rope_half):
    gain_ref, o_ref, acc_ref = rest if len(rest) == 3 else (None,) + rest
    acc_ref[...] = jnp.dot(x_ref[...], w_ref[...].astype(BF16), preferred_element_type=F32)

    def lane_group(j, carry):
        cols = pl.ds(pl.multiple_of(j * LANES, LANES), LANES)
        blk = acc_ref[:, cols]
        if gain_ref is not None:
            blk = _rms_rows(blk) * gain_ref[:, cols]
        out = blk * cos_ref[...] + _rotate_half(blk, rope_half) * sin_ref[...]
        o_ref[:, cols] = out.astype(o_ref.dtype)
        return carry

    lax.fori_loop(0, acc_ref.shape[1] // LANES, lane_group, 0)


def _project(xn, w, layer, col_start, n_cols, rope=None):
    s, d = xn.shape
    first = col_start // PROJ_TN
    in_specs = [pl.BlockSpec((PROJ_TM, d), lambda m, j: (m, 0)),
                pl.BlockSpec((None, d, PROJ_TN), lambda m, j: (layer, 0, first + j))]
    args = [xn, w]
    scratch = []
    if rope is None:
        kernel, name = _proj_plain_kernel, "proj_plain"
    else:
        scratch = [pltpu.VMEM((PROJ_TM, PROJ_TN), F32)]
        cos, sin, rope_half, gain = rope
        kernel = functools.partial(_proj_rope_kernel, rope_half=rope_half)
        name = "proj_rope"
        in_specs += [pl.BlockSpec((PROJ_TM, LANES), lambda m, j: (m, 0)),
                     pl.BlockSpec((PROJ_TM, LANES), lambda m, j: (m, 0))]
        args += [cos, sin]
        if gain is not None:
            name = "proj_qknorm_rope"
            in_specs.append(pl.BlockSpec((1, PROJ_TN), lambda m, j: (0, j)))
            args.append(gain)
    return pl.pallas_call(
        kernel,
        grid=(s // PROJ_TM, n_cols // PROJ_TN),
        in_specs=in_specs,
        out_specs=pl.BlockSpec((PROJ_TM, PROJ_TN), lambda m, j: (m, j)),
        out_shape=jax.ShapeDtypeStruct((s, n_cols), BF16),
        scratch_shapes=scratch,
        compiler_params=_compiler_params(("parallel", "arbitrary")),
        name=name,
    )(*args)


def _retention_tables(ts):
    h = np.arange(RET_HEADS, dtype=np.float64)
    log_g = np.log(1.0 - np.exp2(-5.0 - h))
    idx = np.arange(ts, dtype=np.float64)
    dist = idx[:, None] - idx[None, :]
    scale = RET_DK ** -0.5
    decay = np.where(dist >= 0, np.exp(log_g[:, None, None] * np.maximum(dist, 0.0)), 0.0) * scale
    q_dec = np.exp(log_g[:, None] * (idx + 1.0))
    k_dec = np.exp(log_g[:, None] * (ts - 1.0 - idx)) * scale
    chunk_dec = np.exp(log_g * ts)
    q_dec = np.broadcast_to(q_dec[:, :, None], (RET_HEADS, ts, LANES))
    k_dec = np.broadcast_to(k_dec[:, :, None], (RET_HEADS, ts, LANES))
    return (jnp.asarray(decay, F32), jnp.asarray(q_dec, F32), jnp.asarray(k_dec, F32),
            [float(np.float32(c)) for c in chunk_dec])


def _silu(v):
    return v / (1.0 + jnp.exp(-v))


def _hybrid_kernel(q_ref, k_ref, v_ref, g_ref, cb_ref, cc_ref, cx_ref, convw_ref,
                   decay_ref, qdec_ref, kdec_ref, o_ref, state_ref, u_ref, *, chunk_dec):
    ts = q_ref.shape[0]

    @pl.when(pl.program_id(0) == 0)
    def _():
        state_ref[...] = jnp.zeros_like(state_ref)
        u_ref[0:SUBLANES, :] = jnp.zeros((SUBLANES, u_ref.shape[1]), F32)

    lane = lax.broadcasted_iota(jnp.int32, (ts, LANES), 1)
    for h in range(RET_HEADS):
        pair = slice((h // 2) * LANES, (h // 2 + 1) * LANES)
        head = slice(h * RET_DV, (h + 1) * RET_DV)
        in_head = (lane // RET_DK) == (h % 2)
        qm = jnp.where(in_head, q_ref[:, pair].astype(F32), 0.0)
        kp = k_ref[:, pair]
        vh = v_ref[:, head]
        s = lax.dot_general(qm.astype(BF16), kp, (((1,), (1,)), ((), ())),
                            preferred_element_type=F32)
        inner = (s * decay_ref[h]).astype(BF16)
        qd = (qm * qdec_ref[h]).astype(BF16)
        o = (jnp.dot(inner, vh, preferred_element_type=F32)
             + jnp.dot(qd, state_ref[h].astype(BF16), preferred_element_type=F32))
        kd = (kp.astype(F32) * kdec_ref[h]).astype(BF16)
        state_ref[h] = chunk_dec[h] * state_ref[h] + lax.dot_general(
            kd, vh, (((0,), (0,)), ((), ())), preferred_element_type=F32)
        gate = _silu(g_ref[:, head].astype(F32))
        o_ref[:, head] = (gate * _rms_rows(o)).astype(o_ref.dtype)

    u_ref[SUBLANES:SUBLANES + ts, :] = cc_ref[...].astype(F32) * cx_ref[...].astype(F32)
    y = (convw_ref[0:1, :] * u_ref[SUBLANES - 2:SUBLANES - 2 + ts, :]
         + convw_ref[1:2, :] * u_ref[SUBLANES - 1:SUBLANES - 1 + ts, :]
         + convw_ref[2:3, :] * u_ref[SUBLANES:SUBLANES + ts, :])
    o_ref[:, RET_V:] = (cb_ref[...].astype(F32) * y).astype(o_ref.dtype)
    u_ref[0:SUBLANES, :] = u_ref[ts:ts + SUBLANES, :]


def _hybrid_mix(qk, rest, conv_w):
    s = qk.shape[0]
    ts = RET_TS
    decay, q_dec, k_dec, chunk_dec = _retention_tables(ts)
    qk_blk = RET_QK
    w_blk = RET_V
    const3 = lambda i: (0, 0, 0)
    kernel = functools.partial(_hybrid_kernel, chunk_dec=chunk_dec)
    return pl.pallas_call(
        kernel,
        grid=(s // ts,),
        in_specs=[pl.BlockSpec((ts, qk_blk), lambda i: (i, 0)),
                  pl.BlockSpec((ts, qk_blk), lambda i: (i, 1)),
                  pl.BlockSpec((ts, w_blk), lambda i: (i, 0)),
                  pl.BlockSpec((ts, w_blk), lambda i: (i, 1)),
                  pl.BlockSpec((ts, w_blk), lambda i: (i, 2)),
                  pl.BlockSpec((ts, w_blk), lambda i: (i, 3)),
                  pl.BlockSpec((ts, w_blk), lambda i: (i, 4)),
                  pl.BlockSpec((SUBLANES, CONV_WIDTH), lambda i: (0, 0)),
                  pl.BlockSpec((RET_HEADS, ts, ts), const3),
                  pl.BlockSpec((RET_HEADS, ts, LANES), const3),
                  pl.BlockSpec((RET_HEADS, ts, LANES), const3)],
        out_specs=pl.BlockSpec((ts, RET_V + CONV_WIDTH), lambda i: (i, 0)),
        out_shape=jax.ShapeDtypeStruct((s, RET_V + CONV_WIDTH), BF16),
        scratch_shapes=[pltpu.VMEM((RET_HEADS, LANES, RET_DV), F32),
                        pltpu.VMEM((SUBLANES + ts, CONV_WIDTH), F32)],
        compiler_params=_compiler_params(("arbitrary",)),
        name="retention_conv",
    )(qk, qk, rest, rest, rest, rest, rest,
      jnp.pad(conv_w, ((0, SUBLANES - CONV_K), (0, 0))), decay, q_dec, k_dec)


def _lane_partial_sum(p):
    total = p[:, 0:LANES]
    for t in range(1, p.shape[1] // LANES):
        total = total + p[:, t * LANES:(t + 1) * LANES]
    return total


def _attn_kernel(bound_ref, q_ref, k_ref, v_ref, lq1_ref, lk1_ref, lq2_ref, lk2_ref,
                 subln_ref, o_ref, m_ref, l_ref, acc_ref, *, lambda_init):
    tq = q_ref.shape[0]
    tk = ATT_TK
    d = DIFF_HEAD_DIM
    qi = pl.program_id(1)
    bound = bound_ref[0]

    l_ref[...] = jnp.zeros_like(l_ref)
    acc_ref[...] = jnp.zeros_like(acc_ref)

    def scores(j, i, rows, masked):
        start = pl.multiple_of(j * tk, tk)
        s = lax.dot_general(q_ref[rows, i * d:(i + 1) * d],
                            k_ref[pl.ds(start, tk), i * d:(i + 1) * d],
                            (((1,), (1,)), ((), ())), preferred_element_type=F32)
        if masked:
            row = lax.broadcasted_iota(jnp.int32, s.shape, 0)
            col = lax.broadcasted_iota(jnp.int32, s.shape, 1)
            s = jnp.where(col <= row, s, MASK_VALUE)
        return s, v_ref[pl.ds(start, tk), :]

    def update_bounded(j, rows, masked):
        for i in range(2):
            s, v = scores(j, i, rows, masked)
            p = jnp.exp2(s - bound)
            l_ref[i, rows, :] += _lane_partial_sum(p)
            acc_ref[i, rows, :] += jnp.dot(p.astype(BF16), v, preferred_element_type=F32)

    def update_online(j, rows, masked):
        for i in range(2):
            s, v = scores(j, i, rows, masked)
            m_prev = m_ref[i, rows, :]
            m_new = jnp.maximum(m_prev, jnp.max(s, axis=1, keepdims=True))
            alpha = jnp.exp2(m_prev - m_new)
            p = jnp.exp2(s - m_new[:, :1])
            l_ref[i, rows, :] = alpha * l_ref[i, rows, :] + _lane_partial_sum(p)
            acc_ref[i, rows, :] = alpha[:, :1] * acc_ref[i, rows, :] + jnp.dot(
                p.astype(BF16), v, preferred_element_type=F32)
            m_ref[i, rows, :] = m_new

    def run(update):
        blocks_per_step = tq // tk
        all_rows = slice(0, tq)

        def body(t, carry):
            for r in range(blocks_per_step):
                update(blocks_per_step * t + r, all_rows, masked=False)
            return carry
        lax.fori_loop(0, qi, body, 0)
        for r in range(blocks_per_step):
            update(blocks_per_step * qi + r, slice(r * tk, tq), masked=True)

    @pl.when(bound <= ATT_BOUND_MAX)
    def _():
        run(update_bounded)

    @pl.when(bound > ATT_BOUND_MAX)
    def _():
        m_ref[...] = jnp.full_like(m_ref, MASK_VALUE)
        run(update_online)

    lam = (jnp.exp(jnp.sum(lq1_ref[...] * lk1_ref[...], axis=1, keepdims=True))
           - jnp.exp(jnp.sum(lq2_ref[...] * lk2_ref[...], axis=1, keepdims=True))
           + lambda_init)
    l0 = jnp.sum(l_ref[0], axis=1, keepdims=True)
    l1 = jnp.sum(l_ref[1], axis=1, keepdims=True)
    o = acc_ref[0] / l0 - lam * (acc_ref[1] / l1)
    o = _rms_rows(o) * subln_ref[...] * (1.0 - lambda_init)
    o_ref[...] = o.astype(o_ref.dtype)


def _diff_attention(qk, v, score_bound, lq1, lk1, lq2, lk2, subln, lambda_init):
    s = qk.shape[0]
    hd = 2 * DIFF_HEAD_DIM
    assert ATT_TQ % ATT_TK == 0
    vec = lambda a: a.reshape(1, -1).astype(F32)
    small = lambda w: pl.BlockSpec((1, w), lambda h, i: (0, 0))
    kernel = functools.partial(_attn_kernel, lambda_init=lambda_init)
    return pl.pallas_call(
        kernel,
        grid=(DIFF_HEADS, s // ATT_TQ),
        in_specs=[pl.BlockSpec(memory_space=pltpu.SMEM),
                  pl.BlockSpec((ATT_TQ, hd), lambda h, i: (i, h)),
                  pl.BlockSpec((s, hd), lambda h, i: (0, DIFF_HEADS + h)),
                  pl.BlockSpec((s, hd), lambda h, i: (0, h)),
                  small(DIFF_HEAD_DIM), small(DIFF_HEAD_DIM),
                  small(DIFF_HEAD_DIM), small(DIFF_HEAD_DIM), small(hd)],
        out_specs=pl.BlockSpec((ATT_TQ, hd), lambda h, i: (i, h)),
        out_shape=jax.ShapeDtypeStruct((s, DIFF_HEADS * hd), BF16),
        scratch_shapes=[pltpu.VMEM((2, ATT_TQ, LANES), F32),
                        pltpu.VMEM((2, ATT_TQ, LANES), F32),
                        pltpu.VMEM((2, ATT_TQ, hd), F32)],
        compiler_params=_compiler_params(("parallel", "arbitrary")),
        name="diff_attention",
    )(score_bound, qk, qk, v, vec(lq1), vec(lk1), vec(lq2), vec(lk2), vec(subln))


def _out_kernel(a_ref, w_ref, x_ref, g_ref, xo_ref, xn_ref):
    x = x_ref[...] + jnp.dot(a_ref[...], w_ref[...], preferred_element_type=F32)
    xo_ref[...] = x
    xn_ref[...] = (_rms_rows(x) * g_ref[...]).astype(xn_ref.dtype)


def _out_project(a, w, x, g):
    s, d = x.shape
    row = lambda i: (i, 0)
    return pl.pallas_call(
        _out_kernel,
        grid=(s // OUT_TM,),
        in_specs=[pl.BlockSpec((OUT_TM, a.shape[1]), row),
                  pl.BlockSpec(w.shape, lambda i: (0, 0)),
                  pl.BlockSpec((OUT_TM, d), row),
                  pl.BlockSpec((1, d), lambda i: (0, 0))],
        out_specs=[pl.BlockSpec((OUT_TM, d), row), pl.BlockSpec((OUT_TM, d), row)],
        out_shape=[jax.ShapeDtypeStruct((s, d), F32), jax.ShapeDtypeStruct((s, d), BF16)],
        compiler_params=_compiler_params(("parallel",)),
        name="out_proj_residual_norm",
    )(a, w, x, g.reshape(1, d))


def _ffn_kernel(xn_ref, wg_ref, wu_ref, wd_ref, x_ref, *rest):
    g_ref, xo_ref, xn_out_ref = rest if len(rest) == 3 else (None, rest[0], None)
    f = pl.program_id(1)

    @pl.when(f == 0)
    def _():
        xo_ref[...] = x_ref[...]

    xn = xn_ref[...]
    gate = jnp.dot(xn, wg_ref[...].astype(BF16), preferred_element_type=F32)
    up = jnp.dot(xn, wu_ref[...].astype(BF16), preferred_element_type=F32)
    act = (_silu(gate) * up).astype(BF16)
    xo_ref[...] += jnp.dot(act, wd_ref[...].astype(BF16), preferred_element_type=F32)

    if xn_out_ref is not None:
        @pl.when(f == pl.num_programs(1) - 1)
        def _():
            xn_out_ref[...] = (_rms_rows(xo_ref[...]) * g_ref[...]).astype(xn_out_ref.dtype)


def _ffn(xn, wg, wu, wd, layer, x, g):
    s, d = x.shape
    hidden = wg.shape[2]
    row_block = lambda **kw: pl.BlockSpec((FFN_TM, d), lambda i, f: (i, 0), **kw)
    in_specs = [row_block(),
                pl.BlockSpec((None, d, FFN_TF), lambda i, f: (layer, 0, f)),
                pl.BlockSpec((None, d, FFN_TF), lambda i, f: (layer, 0, f)),
                pl.BlockSpec((None, FFN_TF, d), lambda i, f: (layer, f, 0)),
                row_block()]
    out_specs = [row_block(pipeline_mode=pl.Buffered(1))]
    out_shape = [jax.ShapeDtypeStruct((s, d), F32)]
    args = [xn, wg, wu, wd, x]
    if g is not None:
        in_specs.append(pl.BlockSpec((1, d), lambda i, f: (0, 0)))
        out_specs.append(row_block())
        out_shape.append(jax.ShapeDtypeStruct((s, d), BF16))
        args.append(g.reshape(1, d))
    outs = pl.pallas_call(
        _ffn_kernel,
        grid=(s // FFN_TM, hidden // FFN_TF),
        in_specs=in_specs,
        out_specs=out_specs,
        out_shape=out_shape,
        compiler_params=_compiler_params(("parallel", "arbitrary")),
        name="swiglu_ffn_residual_norm" if g is not None else "swiglu_ffn_residual",
    )(*args)
    return (outs[0], outs[1]) if g is not None else (outs[0], None)


def _rope_tables(seq, dim):
    inv = ROPE_THETA ** (-jnp.arange(0, dim, 2, dtype=F32) / dim)
    ang = jnp.arange(seq, dtype=F32)[:, None] * inv[None, :]
    cos = jnp.cos(ang)
    sin = jnp.sin(ang)
    reps = LANES // dim
    cos_l = jnp.tile(jnp.concatenate([cos, cos], axis=1), (1, reps))
    sin_l = jnp.tile(jnp.concatenate([-sin, sin], axis=1), (1, reps))
    return cos_l, sin_l


def kernel(x, norm_mix, norm_ffn, hyb_w_in, hyb_conv_w, hyb_w_out, diff_w_qkv, diff_q_norm,
           diff_k_norm, diff_lambda_q1, diff_lambda_k1, diff_lambda_q2, diff_lambda_k2,
           diff_subln, diff_w_out, ffn_w_gate, ffn_w_up, ffn_w_down):
    b, s, d = x.shape
    assert b == 1 and d == D_MODEL
    xs = x.reshape(s, d)
    cos_r, sin_r = _rope_tables(s, RET_DK)
    cos_a, sin_a = _rope_tables(s, DIFF_HEAD_DIM)

    xn = _rmsnorm(xs, norm_mix[0])
    for layer in range(DEPTH):
        j = layer // 2
        if layer % 2 == 0:
            qk_cols = 2 * RET_QK
            qk = _project(xn, hyb_w_in, j, 0, qk_cols, rope=(cos_r, sin_r, RET_DK // 2, None))
            rest = _project(xn, hyb_w_in, j, qk_cols, HYB_IN - qk_cols)
            mixed = _hybrid_mix(qk, rest, hyb_conv_w[j])
            w_out = hyb_w_out[j]
        else:
            lambda_init = 0.8 - 0.6 * math.exp(-0.3 * layer)
            n_groups = DIFF_QK_COLS // DIFF_HEAD_DIM
            q_gain = diff_q_norm[j] * (LOG2_E * DIFF_HEAD_DIM ** -0.5)
            gain = jnp.concatenate([jnp.tile(q_gain, n_groups),
                                    jnp.tile(diff_k_norm[j], n_groups)]).reshape(1, -1)
            qk_cols = 2 * DIFF_QK_COLS
            qk = _project(xn, diff_w_qkv, j, 0, qk_cols,
                          rope=(cos_a, sin_a, DIFF_HEAD_DIM // 2, gain))
            v = _project(xn, diff_w_qkv, j, qk_cols, DIFF_QKV - qk_cols)
            score_bound = (ATT_BOUND_SLACK * DIFF_HEAD_DIM * jnp.max(jnp.abs(q_gain))
                           * jnp.max(jnp.abs(diff_k_norm[j]))).reshape(1).astype(F32)
            mixed = _diff_attention(qk, v, score_bound, diff_lambda_q1[j], diff_lambda_k1[j],
                                    diff_lambda_q2[j], diff_lambda_k2[j], diff_subln[j],
                                    lambda_init)
            w_out = diff_w_out[j]
        xs, xn = _out_project(mixed, w_out.astype(BF16), xs, norm_ffn[layer])
        g_next = norm_mix[layer + 1] if layer + 1 < DEPTH else None
        xs, xn = _ffn(xn, ffn_w_gate, ffn_w_up, ffn_w_down, layer, xs, g_next)
    return xs.reshape(b, s, d)
```

```python
import functools
import math

import jax
import jax.numpy as jnp
import numpy as np
from jax import lax
from jax.experimental import pallas as pl
from jax.experimental.pallas import tpu as pltpu

D_MODEL = 2048
DEPTH = 4
ROPE_THETA = 10000.0
NORM_EPS = 1e-6
RET_HEADS = 8
RET_DK = 64
RET_DV = 128
CONV_WIDTH = 1024
CONV_K = 3
DIFF_HEADS = 8
DIFF_HEAD_DIM = 128
FFN_HIDDEN = 5632
RET_QK = RET_HEADS * RET_DK
RET_V = RET_HEADS * RET_DV
HYB_IN = 2 * RET_QK + 2 * RET_V + 3 * CONV_WIDTH
DIFF_QKV = 6144
DIFF_QK_COLS = 2 * DIFF_HEADS * DIFF_HEAD_DIM

LANES = 128
SUBLANES = 8
VMEM_LIMIT_BYTES = 56 * 1024 * 1024

NORM_TM = 512
PROJ_TM = 2048
PROJ_TN = 512
OUT_TM = 512
FFN_TM = 1024
FFN_TF = 256
RET_TS = 256
ATT_TQ = 1024
ATT_TK = 512
MASK_VALUE = -1e30
LOG2_E = math.log2(math.e)
ATT_BOUND_SLACK = 1.02
ATT_BOUND_MAX = 60.0

F32 = jnp.float32
BF16 = jnp.bfloat16


def _compiler_params(semantics):
    return pltpu.CompilerParams(dimension_semantics=semantics,
                                vmem_limit_bytes=VMEM_LIMIT_BYTES)


def _rms_rows(v):
    return v * lax.rsqrt(jnp.mean(v * v, axis=-1, keepdims=True) + NORM_EPS)


def _norm_kernel(x_ref, g_ref, o_ref):
    o_ref[...] = (_rms_rows(x_ref[...]) * g_ref[...]).astype(o_ref.dtype)


def _rmsnorm(x, g):
    s, d = x.shape
    return pl.pallas_call(
        _norm_kernel,
        grid=(s // NORM_TM,),
        in_specs=[pl.BlockSpec((NORM_TM, d), lambda i: (i, 0)),
                  pl.BlockSpec((1, d), lambda i: (0, 0))],
        out_specs=pl.BlockSpec((NORM_TM, d), lambda i: (i, 0)),
        out_shape=jax.ShapeDtypeStruct((s, d), BF16),
        compiler_params=_compiler_params(("parallel",)),
        name="rmsnorm",
    )(x, g.reshape(1, d))


def _rotate_half(blk, half):
    if 2 * half == LANES:
        return pltpu.roll(blk, half, axis=1)
    lane = lax.broadcasted_iota(jnp.int32, blk.shape, 1)
    first = (lane % (2 * half)) < half
    return jnp.where(first, pltpu.roll(blk, LANES - half, axis=1),
                     pltpu.roll(blk, half, axis=1))


def _proj_plain_kernel(x_ref, w_ref, o_ref):
    o_ref[...] = jnp.dot(x_ref[...], w_ref[...].astype(BF16),
                         preferred_element_type=F32).astype(o_ref.dtype)


def _proj_rope_kernel(x_ref, w_ref, cos_ref, sin_ref, *rest, rope_half):
    gain_ref, o_ref, acc_ref = rest if len(rest) == 3 else (None,) + rest
    acc_ref[...] = jnp.dot(x_ref[...], w_ref[...].astype(BF16), preferred_element_type=F32)

    def lane_group(j, carry):
        cols = pl.ds(pl.multiple_of(j * LANES, LANES), LANES)
        blk = acc_ref[:, cols]
        if gain_ref is not None:
            blk = _rms_rows(blk) * gain_ref[:, cols]
        out = blk * cos_ref[...] + _rotate_half(blk, rope_half) * sin_ref[...]
        o_ref[:, cols] = out.astype(o_ref.dtype)
        return carry

    lax.fori_loop(0, acc_ref.shape[1] // LANES, lane_group, 0)


def _project(xn, w, layer, col_start, n_cols, rope=None):
    s, d = xn.shape
    first = col_start // PROJ_TN
    in_specs = [pl.BlockSpec((PROJ_TM, d), lambda m, j: (m, 0)),
                pl.BlockSpec((None, d, PROJ_TN), lambda m, j: (layer, 0, first + j))]
    args = [xn, w]
    scratch = []
    if rope is None:
        kernel, name = _proj_plain_kernel, "proj_plain"
    else:
        scratch = [pltpu.VMEM((PROJ_TM, PROJ_TN), F32)]
        cos, sin, rope_half, gain = rope
        kernel = functools.partial(_proj_rope_kernel, rope_half=rope_half)
        name = "proj_rope"
        in_specs += [pl.BlockSpec((PROJ_TM, LANES), lambda m, j: (m, 0)),
                     pl.BlockSpec((PROJ_TM, LANES), lambda m, j: (m, 0))]
        args += [cos, sin]
        if gain is not None:
            name = "proj_qknorm_rope"
            in_specs.append(pl.BlockSpec((1, PROJ_TN), lambda m, j: (0, j)))
            args.append(gain)
    return pl.pallas_call(
        kernel,
        grid=(s // PROJ_TM, n_cols // PROJ_TN),
        in_specs=in_specs,
        out_specs=pl.BlockSpec((PROJ_TM, PROJ_TN), lambda m, j: (m, j)),
        out_shape=jax.ShapeDtypeStruct((s, n_cols), BF16),
        scratch_shapes=scratch,
        compiler_params=_compiler_params(("parallel", "arbitrary")),
        name=name,
    )(*args)


def _retention_tables(ts):
    h = np.arange(RET_HEADS, dtype=np.float64)
    log_g = np.log(1.0 - np.exp2(-5.0 - h))
    idx = np.arange(ts, dtype=np.float64)
    dist = idx[:, None] - idx[None, :]
    scale = RET_DK ** -0.5
    decay = np.where(dist >= 0, np.exp(log_g[:, None, None] * np.maximum(dist, 0.0)), 0.0) * scale
    q_dec = np.exp(log_g[:, None] * (idx + 1.0))
    k_dec = np.exp(log_g[:, None] * (ts - 1.0 - idx)) * scale
    chunk_dec = np.exp(log_g * ts)
    q_dec = np.broadcast_to(q_dec[:, :, None], (RET_HEADS, ts, LANES))
    k_dec = np.broadcast_to(k_dec[:, :, None], (RET_HEADS, ts, LANES))
    return (jnp.asarray(decay, F32), jnp.asarray(q_dec, F32), jnp.asarray(k_dec, F32),
            [float(np.float32(c)) for c in chunk_dec])


def _silu(v):
    return v / (1.0 + jnp.exp(-v))


def _hybrid_kernel(q_ref, k_ref, v_ref, g_ref, cb_ref, cc_ref, cx_ref, convw_ref,
                   decay_ref, qdec_ref, kdec_ref, o_ref, state_ref, u_ref, *, chunk_dec):
    ts = q_ref.shape[0]

    @pl.when(pl.program_id(0) == 0)
    def _():
        state_ref[...] = jnp.zeros_like(state_ref)
        u_ref[0:SUBLANES, :] = jnp.zeros((SUBLANES, u_ref.shape[1]), F32)

    lane = lax.broadcasted_iota(jnp.int32, (ts, LANES), 1)
    for h in range(RET_HEADS):
        pair = slice((h // 2) * LANES, (h // 2 + 1) * LANES)
        head = slice(h * RET_DV, (h + 1) * RET_DV)
        in_head = (lane // RET_DK) == (h % 2)
        qm = jnp.where(in_head, q_ref[:, pair].astype(F32), 0.0)
        kp = k_ref[:, pair]
        vh = v_ref[:, head]
        s = lax.dot_general(qm.astype(BF16), kp, (((1,), (1,)), ((), ())),
                            preferred_element_type=F32)
        inner = (s * decay_ref[h]).astype(BF16)
        qd = (qm * qdec_ref[h]).astype(BF16)
        o = (jnp.dot(inner, vh, preferred_element_type=F32)
             + jnp.dot(qd, state_ref[h].astype(BF16), preferred_element_type=F32))
        kd = (kp.astype(F32) * kdec_ref[h]).astype(BF16)
        state_ref[h] = chunk_dec[h] * state_ref[h] + lax.dot_general(
            kd, vh, (((0,), (0,)), ((), ())), preferred_element_type=F32)
        gate = _silu(g_ref[:, head].astype(F32))
        o_ref[:, head] = (gate * _rms_rows(o)).astype(o_ref.dtype)

    u_ref[SUBLANES:SUBLANES + ts, :] = cc_ref[...].astype(F32) * cx_ref[...].astype(F32)
    y = (convw_ref[0:1, :] * u_ref[SUBLANES - 2:SUBLANES - 2 + ts, :]
         + convw_ref[1:2, :] * u_ref[SUBLANES - 1:SUBLANES - 1 + ts, :]
         + convw_ref[2:3, :] * u_ref[SUBLANES:SUBLANES + ts, :])
    o_ref[:, RET_V:] = (cb_ref[...].astype(F32) * y).astype(o_ref.dtype)
    u_ref[0:SUBLANES, :] = u_ref[ts:ts + SUBLANES, :]


def _hybrid_mix(qk, rest, conv_w):
    s = qk.shape[0]
    ts = RET_TS
    decay, q_dec, k_dec, chunk_dec = _retention_tables(ts)
    qk_blk = RET_QK
    w_blk = RET_V
    const3 = lambda i: (0, 0, 0)
    kernel = functools.partial(_hybrid_kernel, chunk_dec=chunk_dec)
    return pl.pallas_call(
        kernel,
        grid=(s // ts,),
        in_specs=[pl.BlockSpec((ts, qk_blk), lambda i: (i, 0)),
                  pl.BlockSpec((ts, qk_blk), lambda i: (i, 1)),
                  pl.BlockSpec((ts, w_blk), lambda i: (i, 0)),
                  pl.BlockSpec((ts, w_blk), lambda i: (i, 1)),
                  pl.BlockSpec((ts, w_blk), lambda i: (i, 2)),
                  pl.BlockSpec((ts, w_blk), lambda i: (i, 3)),
                  pl.BlockSpec((ts, w_blk), lambda i: (i, 4)),
                  pl.BlockSpec((SUBLANES, CONV_WIDTH), lambda i: (0, 0)),
                  pl.BlockSpec((RET_HEADS, ts, ts), const3),
                  pl.BlockSpec((RET_HEADS, ts, LANES), const3),
                  pl.BlockSpec((RET_HEADS, ts, LANES), const3)],
        out_specs=pl.BlockSpec((ts, RET_V + CONV_WIDTH), lambda i: (i, 0)),
        out_shape=jax.ShapeDtypeStruct((s, RET_V + CONV_WIDTH), BF16),
        scratch_shapes=[pltpu.VMEM((RET_HEADS, LANES, RET_DV), F32),
                        pltpu.VMEM((SUBLANES + ts, CONV_WIDTH), F32)],
        compiler_params=_compiler_params(("arbitrary",)),
        name="retention_conv",
    )(qk, qk, rest, rest, rest, rest, rest,
      jnp.pad(conv_w, ((0, SUBLANES - CONV_K), (0, 0))), decay, q_dec, k_dec)


def _lane_partial_sum(p):
    total = p[:, 0:LANES]
    for t in range(1, p.shape[1] // LANES):
        total = total + p[:, t * LANES:(t + 1) * LANES]
    return total


def _attn_kernel(bound_ref, q_ref, k_ref, v_ref, lq1_ref, lk1_ref, lq2_ref, lk2_ref,
                 subln_ref, o_ref, m_ref, l_ref, acc_ref, *, lambda_init):
    tq = q_ref.shape[0]
    tk = ATT_TK
    d = DIFF_HEAD_DIM
    qi = pl.program_id(1)
    bound = bound_ref[0]

    l_ref[...] = jnp.zeros_like(l_ref)
    acc_ref[...] = jnp.zeros_like(acc_ref)

    def scores(j, i, rows, masked):
        start = pl.multiple_of(j * tk, tk)
        s = lax.dot_general(q_ref[rows, i * d:(i + 1) * d],
                            k_ref[pl.ds(start, tk), i * d:(i + 1) * d],
                            (((1,), (1,)), ((), ())), preferred_element_type=F32)
        if masked:
            row = lax.broadcasted_iota(jnp.int32, s.shape, 0)
            col = lax.broadcasted_iota(jnp.int32, s.shape, 1)
            s = jnp.where(col <= row, s, MASK_VALUE)
        return s, v_ref[pl.ds(start, tk), :]

    def update_bounded(j, rows, masked):
        for i in range(2):
            s, v = scores(j, i, rows, masked)
            p = jnp.exp2(s - bound)
            l_ref[i, rows, :] += _lane_partial_sum(p)
            acc_ref[i, rows, :] += jnp.dot(p.astype(BF16), v, preferred_element_type=F32)

    def update_online(j, rows, masked):
        for i in range(2):
            s, v = scores(j, i, rows, masked)
            m_prev = m_ref[i, rows, :]
            m_new = jnp.maximum(m_prev, jnp.max(s, axis=1, keepdims=True))
            alpha = jnp.exp2(m_prev - m_new)
            p = jnp.exp2(s - m_new[:, :1])
            l_ref[i, rows, :] = alpha * l_ref[i, rows, :] + _lane_partial_sum(p)
            acc_ref[i, rows, :] = alpha[:, :1] * acc_ref[i, rows, :] + jnp.dot(
                p.astype(BF16), v, preferred_element_type=F32)
            m_ref[i, rows, :] = m_new

    def run(update):
        blocks_per_step = tq // tk
        all_rows = slice(0, tq)

        def body(t, carry):
            for r in range(blocks_per_step):
                update(blocks_per_step * t + r, all_rows, masked=False)
            return carry
        lax.fori_loop(0, qi, body, 0)
        for r in range(blocks_per_step):
            update(blocks_per_step * qi + r, slice(r * tk, tq), masked=True)

    @pl.when(bound <= ATT_BOUND_MAX)
    def _():
        run(update_bounded)

    @pl.when(bound > ATT_BOUND_MAX)
    def _():
        m_ref[...] = jnp.full_like(m_ref, MASK_VALUE)
        run(update_online)

    lam = (jnp.exp(jnp.sum(lq1_ref[...] * lk1_ref[...], axis=1, keepdims=True))
           - jnp.exp(jnp.sum(lq2_ref[...] * lk2_ref[...], axis=1, keepdims=True))
           + lambda_init)
    l0 = jnp.sum(l_ref[0], axis=1, keepdims=True)
    l1 = jnp.sum(l_ref[1], axis=1, keepdims=True)
    o = acc_ref[0] / l0 - lam * (acc_ref[1] / l1)
    o = _rms_rows(o) * subln_ref[...] * (1.0 - lambda_init)
    o_ref[...] = o.astype(o_ref.dtype)


def _diff_attention(qk, v, score_bound, lq1, lk1, lq2, lk2, subln, lambda_init):
    s = qk.shape[0]
    hd = 2 * DIFF_HEAD_DIM
    assert ATT_TQ % ATT_TK == 0
    vec = lambda a: a.reshape(1, -1).astype(F32)
    small = lambda w: pl.BlockSpec((1, w), lambda h, i: (0, 0))
    kernel = functools.partial(_attn_kernel, lambda_init=lambda_init)
    return pl.pallas_call(
        kernel,
        grid=(DIFF_HEADS, s // ATT_TQ),
        in_specs=[pl.BlockSpec(memory_space=pltpu.SMEM),
                  pl.BlockSpec((ATT_TQ, hd), lambda h, i: (i, h)),
                  pl.BlockSpec((s, hd), lambda h, i: (0, DIFF_HEADS + h)),
                  pl.BlockSpec((s, hd), lambda h, i: (0, h)),
                  small(DIFF_HEAD_DIM), small(DIFF_HEAD_DIM),
                  small(DIFF_HEAD_DIM), small(DIFF_HEAD_DIM), small(hd)],
        out_specs=pl.BlockSpec((ATT_TQ, hd), lambda h, i: (i, h)),
        out_shape=jax.ShapeDtypeStruct((s, DIFF_HEADS * hd), BF16),
        scratch_shapes=[pltpu.VMEM((2, ATT_TQ, LANES), F32),
                        pltpu.VMEM((2, ATT_TQ, LANES), F32),
                        pltpu.VMEM((2, ATT_TQ, hd), F32)],
        compiler_params=_compiler_params(("parallel", "arbitrary")),
        name="diff_attention",
    )(score_bound, qk, qk, v, vec(lq1), vec(lk1), vec(lq2), vec(lk2), vec(subln))


def _out_kernel(a_ref, w_ref, x_ref, g_ref, xo_ref, xn_ref, wb_ref):
    @pl.when(pl.program_id(0) == 0)
    def _():
        wb_ref[...] = w_ref[...].astype(BF16)

    x = x_ref[...] + jnp.dot(a_ref[...], wb_ref[...], preferred_element_type=F32)
    xo_ref[...] = x
    xn_ref[...] = (_rms_rows(x) * g_ref[...]).astype(xn_ref.dtype)


def _out_project(a, w, layer, x, g):
    s, d = x.shape
    k = a.shape[1]
    row = lambda i: (i, 0)
    return pl.pallas_call(
        _out_kernel,
        grid=(s // OUT_TM,),
        in_specs=[pl.BlockSpec((OUT_TM, k), row),
                  pl.BlockSpec((None, k, d), lambda i: (layer, 0, 0),
                               pipeline_mode=pl.Buffered(1)),
                  pl.BlockSpec((OUT_TM, d), row),
                  pl.BlockSpec((1, d), lambda i: (0, 0))],
        out_specs=[pl.BlockSpec((OUT_TM, d), row), pl.BlockSpec((OUT_TM, d), row)],
        out_shape=[jax.ShapeDtypeStruct((s, d), F32), jax.ShapeDtypeStruct((s, d), BF16)],
        scratch_shapes=[pltpu.VMEM((k, d), BF16)],
        compiler_params=_compiler_params(("arbitrary",)),
        name="out_proj_residual_norm",
    )(a, w, x, g.reshape(1, d))


def _ffn_kernel(xn_ref, wg_ref, wu_ref, wd_ref, x_ref, *rest):
    g_ref, xo_ref, xn_out_ref = rest if len(rest) == 3 else (None, rest[0], None)
    f = pl.program_id(1)

    @pl.when(f == 0)
    def _():
        xo_ref[...] = x_ref[...]

    xn = xn_ref[...]
    gate = jnp.dot(xn, wg_ref[...].astype(BF16), preferred_element_type=F32)
    up = jnp.dot(xn, wu_ref[...].astype(BF16), preferred_element_type=F32)
    act = (_silu(gate) * up).astype(BF16)
    xo_ref[...] += jnp.dot(act, wd_ref[...].astype(BF16), preferred_element_type=F32)

    if xn_out_ref is not None:
        @pl.when(f == pl.num_programs(1) - 1)
        def _():
            xn_out_ref[...] = (_rms_rows(xo_ref[...]) * g_ref[...]).astype(xn_out_ref.dtype)


def _ffn(xn, wg, wu, wd, layer, x, g):
    s, d = x.shape
    hidden = wg.shape[2]
    row_block = lambda **kw: pl.BlockSpec((FFN_TM, d), lambda i, f: (i, 0), **kw)
    in_specs = [row_block(),
                pl.BlockSpec((None, d, FFN_TF), lambda i, f: (layer, 0, f)),
                pl.BlockSpec((None, d, FFN_TF), lambda i, f: (layer, 0, f)),
                pl.BlockSpec((None, FFN_TF, d), lambda i, f: (layer, f, 0)),
                row_block()]
    out_specs = [row_block(pipeline_mode=pl.Buffered(1))]
    out_shape = [jax.ShapeDtypeStruct((s, d), F32)]
    args = [xn, wg, wu, wd, x]
    if g is not None:
        in_specs.append(pl.BlockSpec((1, d), lambda i, f: (0, 0)))
        out_specs.append(row_block())
        out_shape.append(jax.ShapeDtypeStruct((s, d), BF16))
        args.append(g.reshape(1, d))
    outs = pl.pallas_call(
        _ffn_kernel,
        grid=(s // FFN_TM, hidden // FFN_TF),
        in_specs=in_specs,
        out_specs=out_specs,
        out_shape=out_shape,
        compiler_params=_compiler_params(("parallel", "arbitrary")),
        name="swiglu_ffn_residual_norm" if g is not None else "swiglu_ffn_residual",
    )(*args)
    return (outs[0], outs[1]) if g is not None else (outs[0], None)


def _rope_tables(seq, dim):
    inv = ROPE_THETA ** (-jnp.arange(0, dim, 2, dtype=F32) / dim)
    ang = jnp.arange(seq, dtype=F32)[:, None] * inv[None, :]
    cos = jnp.cos(ang)
    sin = jnp.sin(ang)
    reps = LANES // dim
    cos_l = jnp.tile(jnp.concatenate([cos, cos], axis=1), (1, reps))
    sin_l = jnp.tile(jnp.concatenate([-sin, sin], axis=1), (1, reps))
    return cos_l, sin_l


def kernel(x, norm_mix, norm_ffn, hyb_w_in, hyb_conv_w, hyb_w_out, diff_w_qkv, diff_q_norm,
           diff_k_norm, diff_lambda_q1, diff_lambda_k1, diff_lambda_q2, diff_lambda_k2,
           diff_subln, diff_w_out, ffn_w_gate, ffn_w_up, ffn_w_down):
    b, s, d = x.shape
    assert b == 1 and d == D_MODEL
    xs = x.reshape(s, d)
    cos_r, sin_r = _rope_tables(s, RET_DK)
    cos_a, sin_a = _rope_tables(s, DIFF_HEAD_DIM)

    xn = _rmsnorm(xs, norm_mix[0])
    for layer in range(DEPTH):
        j = layer // 2
        if layer % 2 == 0:
            qk_cols = 2 * RET_QK
            qk = _project(xn, hyb_w_in, j, 0, qk_cols, rope=(cos_r, sin_r, RET_DK // 2, None))
            rest = _project(xn, hyb_w_in, j, qk_cols, HYB_IN - qk_cols)
            mixed = _hybrid_mix(qk, rest, hyb_conv_w[j])
            w_out = hyb_w_out
        else:
            lambda_init = 0.8 - 0.6 * math.exp(-0.3 * layer)
            n_groups = DIFF_QK_COLS // DIFF_HEAD_DIM
            q_gain = diff_q_norm[j] * (LOG2_E * DIFF_HEAD_DIM ** -0.5)
            gain = jnp.concatenate([jnp.tile(q_gain, n_groups),
                                    jnp.tile(diff_k_norm[j], n_groups)]).reshape(1, -1)
            qk_cols = 2 * DIFF_QK_COLS
            qk = _project(xn, diff_w_qkv, j, 0, qk_cols,
                          rope=(cos_a, sin_a, DIFF_HEAD_DIM // 2, gain))
            v = _project(xn, diff_w_qkv, j, qk_cols, DIFF_QKV - qk_cols)
            score_bound = (ATT_BOUND_SLACK * DIFF_HEAD_DIM * jnp.max(jnp.abs(q_gain))
                           * jnp.max(jnp.abs(diff_k_norm[j]))).reshape(1).astype(F32)
            mixed = _diff_attention(qk, v, score_bound, diff_lambda_q1[j], diff_lambda_k1[j],
                                    diff_lambda_q2[j], diff_lambda_k2[j], diff_subln[j],
                                    lambda_init)
            w_out = diff_w_out
        xs, xn = _out_project(mixed, w_out, j, xs, norm_ffn[layer])
        g_next = norm_mix[layer + 1] if layer + 1 < DEPTH else None
        xs, xn = _ffn(xn, ffn_w_gate, ffn_w_up, ffn_w_down, layer, xs, g_next)
    return xs.reshape(b, s, d)
```

```python
import functools
import math

import jax
import jax.numpy as jnp
import numpy as np
from jax import lax
from jax.experimental import pallas as pl
from jax.experimental.pallas import tpu as pltpu

D_MODEL = 2048
DEPTH = 4
ROPE_THETA = 10000.0
NORM_EPS = 1e-6
RET_HEADS = 8
RET_DK = 64
RET_DV = 128
CONV_WIDTH = 1024
CONV_K = 3
DIFF_HEADS = 8
DIFF_HEAD_DIM = 128
FFN_HIDDEN = 5632
RET_QK = RET_HEADS * RET_DK
RET_V = RET_HEADS * RET_DV
HYB_IN = 2 * RET_QK + 2 * RET_V + 3 * CONV_WIDTH
DIFF_QKV = 6144
DIFF_QK_COLS = 2 * DIFF_HEADS * DIFF_HEAD_DIM

LANES = 128
SUBLANES = 8
VMEM_LIMIT_BYTES = 56 * 1024 * 1024

NORM_TM = 512
PROJ_TM = 2048
PROJ_TN = 512
OUT_TM = 512
FFN_TM = 1024
FFN_TF = 256
RET_TS = 256
ATT_TQ = 2048
ATT_TK = 512
ATT_UNROLL = 2
MASK_VALUE = -1e30
LOG2_E = math.log2(math.e)
ATT_BOUND_SLACK = 1.02
ATT_BOUND_MAX = 60.0

F32 = jnp.float32
BF16 = jnp.bfloat16


def _compiler_params(semantics):
    return pltpu.CompilerParams(dimension_semantics=semantics,
                                vmem_limit_bytes=VMEM_LIMIT_BYTES)


def _rms_rows(v):
    return v * lax.rsqrt(jnp.mean(v * v, axis=-1, keepdims=True) + NORM_EPS)


def _norm_kernel(x_ref, g_ref, o_ref):
    o_ref[...] = (_rms_rows(x_ref[...]) * g_ref[...]).astype(o_ref.dtype)


def _rmsnorm(x, g):
    s, d = x.shape
    return pl.pallas_call(
        _norm_kernel,
        grid=(s // NORM_TM,),
        in_specs=[pl.BlockSpec((NORM_TM, d), lambda i: (i, 0)),
                  pl.BlockSpec((1, d), lambda i: (0, 0))],
        out_specs=pl.BlockSpec((NORM_TM, d), lambda i: (i, 0)),
        out_shape=jax.ShapeDtypeStruct((s, d), BF16),
        compiler_params=_compiler_params(("parallel",)),
        name="rmsnorm",
    )(x, g.reshape(1, d))


def _rotate_half(blk, half):
    if 2 * half == LANES:
        return pltpu.roll(blk, half, axis=1)
    lane = lax.broadcasted_iota(jnp.int32, blk.shape, 1)
    first = (lane % (2 * half)) < half
    return jnp.where(first, pltpu.roll(blk, LANES - half, axis=1),
                     pltpu.roll(blk, half, axis=1))


def _proj_plain_kernel(x_ref, w_ref, o_ref):
    o_ref[...] = jnp.dot(x_ref[...], w_ref[...].astype(BF16),
                         preferred_element_type=F32).astype(o_ref.dtype)


def _proj_rope_kernel(x_ref, w_ref, cos_ref, sin_ref, *rest, rope_half):
    gain_ref, o_ref, acc_ref = rest if len(rest) == 3 else (None,) + rest
    acc_ref[...] = jnp.dot(x_ref[...], w_ref[...].astype(BF16), preferred_element_type=F32)

    def lane_group(j, carry):
        cols = pl.ds(pl.multiple_of(j * LANES, LANES), LANES)
        blk = acc_ref[:, cols]
        if gain_ref is not None:
            blk = _rms_rows(blk) * gain_ref[:, cols]
        out = blk * cos_ref[...] + _rotate_half(blk, rope_half) * sin_ref[...]
        o_ref[:, cols] = out.astype(o_ref.dtype)
        return carry

    lax.fori_loop(0, acc_ref.shape[1] // LANES, lane_group, 0)


def _project(xn, w, layer, col_start, n_cols, rope=None):
    s, d = xn.shape
    first = col_start // PROJ_TN
    in_specs = [pl.BlockSpec((PROJ_TM, d), lambda m, j: (m, 0)),
                pl.BlockSpec((None, d, PROJ_TN), lambda m, j: (layer, 0, first + j))]
    args = [xn, w]
    scratch = []
    if rope is None:
        kernel, name = _proj_plain_kernel, "proj_plain"
    else:
        scratch = [pltpu.VMEM((PROJ_TM, PROJ_TN), F32)]
        cos, sin, rope_half, gain = rope
        kernel = functools.partial(_proj_rope_kernel, rope_half=rope_half)
        name = "proj_rope"
        in_specs += [pl.BlockSpec((PROJ_TM, LANES), lambda m, j: (m, 0)),
                     pl.BlockSpec((PROJ_TM, LANES), lambda m, j: (m, 0))]
        args += [cos, sin]
        if gain is not None:
            name = "proj_qknorm_rope"
            in_specs.append(pl.BlockSpec((1, PROJ_TN), lambda m, j: (0, j)))
            args.append(gain)
    return pl.pallas_call(
        kernel,
        grid=(s // PROJ_TM, n_cols // PROJ_TN),
        in_specs=in_specs,
        out_specs=pl.BlockSpec((PROJ_TM, PROJ_TN), lambda m, j: (m, j)),
        out_shape=jax.ShapeDtypeStruct((s, n_cols), BF16),
        scratch_shapes=scratch,
        compiler_params=_compiler_params(("parallel", "arbitrary")),
        name=name,
    )(*args)


def _retention_tables(ts):
    h = np.arange(RET_HEADS, dtype=np.float64)
    log_g = np.log(1.0 - np.exp2(-5.0 - h))
    idx = np.arange(ts, dtype=np.float64)
    dist = idx[:, None] - idx[None, :]
    scale = RET_DK ** -0.5
    decay = np.where(dist >= 0, np.exp(log_g[:, None, None] * np.maximum(dist, 0.0)), 0.0) * scale
    q_dec = np.exp(log_g[:, None] * (idx + 1.0))
    k_dec = np.exp(log_g[:, None] * (ts - 1.0 - idx)) * scale
    chunk_dec = np.exp(log_g * ts)
    q_dec = np.broadcast_to(q_dec[:, :, None], (RET_HEADS, ts, LANES))
    k_dec = np.broadcast_to(k_dec[:, :, None], (RET_HEADS, ts, LANES))
    return (jnp.asarray(decay, F32), jnp.asarray(q_dec, F32), jnp.asarray(k_dec, F32),
            [float(np.float32(c)) for c in chunk_dec])


def _silu(v):
    return v / (1.0 + jnp.exp(-v))


def _hybrid_kernel(q_ref, k_ref, v_ref, g_ref, cb_ref, cc_ref, cx_ref, convw_ref,
                   decay_ref, qdec_ref, kdec_ref, o_ref, state_ref, u_ref, *, chunk_dec):
    ts = q_ref.shape[0]

    @pl.when(pl.program_id(0) == 0)
    def _():
        state_ref[...] = jnp.zeros_like(state_ref)
        u_ref[0:SUBLANES, :] = jnp.zeros((SUBLANES, u_ref.shape[1]), F32)

    lane = lax.broadcasted_iota(jnp.int32, (ts, LANES), 1)
    for h in range(RET_HEADS):
        pair = slice((h // 2) * LANES, (h // 2 + 1) * LANES)
        head = slice(h * RET_DV, (h + 1) * RET_DV)
        in_head = (lane // RET_DK) == (h % 2)
        qm = jnp.where(in_head, q_ref[:, pair].astype(F32), 0.0)
        kp = k_ref[:, pair]
        vh = v_ref[:, head]
        s = lax.dot_general(qm.astype(BF16), kp, (((1,), (1,)), ((), ())),
                            preferred_element_type=F32)
        inner = (s * decay_ref[h]).astype(BF16)
        qd = (qm * qdec_ref[h]).astype(BF16)
        o = (jnp.dot(inner, vh, preferred_element_type=F32)
             + jnp.dot(qd, state_ref[h].astype(BF16), preferred_element_type=F32))
        kd = (kp.astype(F32) * kdec_ref[h]).astype(BF16)
        state_ref[h] = chunk_dec[h] * state_ref[h] + lax.dot_general(
            kd, vh, (((0,), (0,)), ((), ())), preferred_element_type=F32)
        gate = _silu(g_ref[:, head].astype(F32))
        o_ref[:, head] = (gate * _rms_rows(o)).astype(o_ref.dtype)

    u_ref[SUBLANES:SUBLANES + ts, :] = cc_ref[...].astype(F32) * cx_ref[...].astype(F32)
    y = (convw_ref[0:1, :] * u_ref[SUBLANES - 2:SUBLANES - 2 + ts, :]
         + convw_ref[1:2, :] * u_ref[SUBLANES - 1:SUBLANES - 1 + ts, :]
         + convw_ref[2:3, :] * u_ref[SUBLANES:SUBLANES + ts, :])
    o_ref[:, RET_V:] = (cb_ref[...].astype(F32) * y).astype(o_ref.dtype)
    u_ref[0:SUBLANES, :] = u_ref[ts:ts + SUBLANES, :]


def _hybrid_mix(qk, rest, conv_w):
    s = qk.shape[0]
    ts = RET_TS
    decay, q_dec, k_dec, chunk_dec = _retention_tables(ts)
    qk_blk = RET_QK
    w_blk = RET_V
    const3 = lambda i: (0, 0, 0)
    kernel = functools.partial(_hybrid_kernel, chunk_dec=chunk_dec)
    return pl.pallas_call(
        kernel,
        grid=(s // ts,),
        in_specs=[pl.BlockSpec((ts, qk_blk), lambda i: (i, 0)),
                  pl.BlockSpec((ts, qk_blk), lambda i: (i, 1)),
                  pl.BlockSpec((ts, w_blk), lambda i: (i, 0)),
                  pl.BlockSpec((ts, w_blk), lambda i: (i, 1)),
                  pl.BlockSpec((ts, w_blk), lambda i: (i, 2)),
                  pl.BlockSpec((ts, w_blk), lambda i: (i, 3)),
                  pl.BlockSpec((ts, w_blk), lambda i: (i, 4)),
                  pl.BlockSpec((SUBLANES, CONV_WIDTH), lambda i: (0, 0)),
                  pl.BlockSpec((RET_HEADS, ts, ts), const3),
                  pl.BlockSpec((RET_HEADS, ts, LANES), const3),
                  pl.BlockSpec((RET_HEADS, ts, LANES), const3)],
        out_specs=pl.BlockSpec((ts, RET_V + CONV_WIDTH), lambda i: (i, 0)),
        out_shape=jax.ShapeDtypeStruct((s, RET_V + CONV_WIDTH), BF16),
        scratch_shapes=[pltpu.VMEM((RET_HEADS, LANES, RET_DV), F32),
                        pltpu.VMEM((SUBLANES + ts, CONV_WIDTH), F32)],
        compiler_params=_compiler_params(("arbitrary",)),
        name="retention_conv",
    )(qk, qk, rest, rest, rest, rest, rest,
      jnp.pad(conv_w, ((0, SUBLANES - CONV_K), (0, 0))), decay, q_dec, k_dec)


def _lane_partial_sum(p):
    total = p[:, 0:LANES]
    for t in range(1, p.shape[1] // LANES):
        total = total + p[:, t * LANES:(t + 1) * LANES]
    return total


def _attn_kernel(bound_ref, q_ref, k_ref, v_ref, lq1_ref, lk1_ref, lq2_ref, lk2_ref,
                 subln_ref, o_ref, m_ref, l_ref, acc_ref, *, lambda_init):
    tq = q_ref.shape[0]
    tk = ATT_TK
    d = DIFF_HEAD_DIM
    qi = pl.program_id(1)
    bound = bound_ref[0]

    l_ref[...] = jnp.zeros_like(l_ref)
    acc_ref[...] = jnp.zeros_like(acc_ref)

    def scores(j, i, rows, masked):
        start = pl.multiple_of(j * tk, tk)
        s = lax.dot_general(q_ref[rows, i * d:(i + 1) * d],
                            k_ref[pl.ds(start, tk), i * d:(i + 1) * d],
                            (((1,), (1,)), ((), ())), preferred_element_type=F32)
        if masked:
            row = lax.broadcasted_iota(jnp.int32, s.shape, 0)
            col = lax.broadcasted_iota(jnp.int32, s.shape, 1)
            s = jnp.where(col <= row, s, MASK_VALUE)
        return s, v_ref[pl.ds(start, tk), :]

    def update_bounded(j, rows, masked):
        for i in range(2):
            s, v = scores(j, i, rows, masked)
            p = jnp.exp2(s - bound)
            l_ref[i, rows, :] += _lane_partial_sum(p)
            acc_ref[i, rows, :] += jnp.dot(p.astype(BF16), v, preferred_element_type=F32)

    def update_online(j, rows, masked):
        for i in range(2):
            s, v = scores(j, i, rows, masked)
            m_prev = m_ref[i, rows, :]
            m_new = jnp.maximum(m_prev, jnp.max(s, axis=1, keepdims=True))
            alpha = jnp.exp2(m_prev - m_new)
            p = jnp.exp2(s - m_new[:, :1])
            l_ref[i, rows, :] = alpha * l_ref[i, rows, :] + _lane_partial_sum(p)
            acc_ref[i, rows, :] = alpha[:, :1] * acc_ref[i, rows, :] + jnp.dot(
                p.astype(BF16), v, preferred_element_type=F32)
            m_ref[i, rows, :] = m_new

    def run(update):
        blocks_per_step = tq // tk
        all_rows = slice(0, tq)

        def body(t, carry):
            for u in range(ATT_UNROLL):
                update(ATT_UNROLL * t + u, all_rows, masked=False)
            return carry
        lax.fori_loop(0, (blocks_per_step // ATT_UNROLL) * qi, body, 0)
        for r in range(blocks_per_step):
            update(blocks_per_step * qi + r, slice(r * tk, tq), masked=True)

    @pl.when(bound <= ATT_BOUND_MAX)
    def _():
        run(update_bounded)

    @pl.when(bound > ATT_BOUND_MAX)
    def _():
        m_ref[...] = jnp.full_like(m_ref, MASK_VALUE)
        run(update_online)

    lam = (jnp.exp(jnp.sum(lq1_ref[...] * lk1_ref[...], axis=1, keepdims=True))
           - jnp.exp(jnp.sum(lq2_ref[...] * lk2_ref[...], axis=1, keepdims=True))
           + lambda_init)
    l0 = jnp.sum(l_ref[0], axis=1, keepdims=True)
    l1 = jnp.sum(l_ref[1], axis=1, keepdims=True)
    o = acc_ref[0] / l0 - lam * (acc_ref[1] / l1)
    o = _rms_rows(o) * subln_ref[...] * (1.0 - lambda_init)
    o_ref[...] = o.astype(o_ref.dtype)


def _diff_attention(qk, v, score_bound, lq1, lk1, lq2, lk2, subln, lambda_init):
    s = qk.shape[0]
    hd = 2 * DIFF_HEAD_DIM
    assert ATT_TQ % ATT_TK == 0
    vec = lambda a: a.reshape(1, -1).astype(F32)
    small = lambda w: pl.BlockSpec((1, w), lambda h, i: (0, 0))
    kernel = functools.partial(_attn_kernel, lambda_init=lambda_init)
    return pl.pallas_call(
        kernel,
        grid=(DIFF_HEADS, s // ATT_TQ),
        in_specs=[pl.BlockSpec(memory_space=pltpu.SMEM),
                  pl.BlockSpec((ATT_TQ, hd), lambda h, i: (i, h)),
                  pl.BlockSpec((s, hd), lambda h, i: (0, DIFF_HEADS + h)),
                  pl.BlockSpec((s, hd), lambda h, i: (0, h)),
                  small(DIFF_HEAD_DIM), small(DIFF_HEAD_DIM),
                  small(DIFF_HEAD_DIM), small(DIFF_HEAD_DIM), small(hd)],
        out_specs=pl.BlockSpec((ATT_TQ, hd), lambda h, i: (i, h)),
        out_shape=jax.ShapeDtypeStruct((s, DIFF_HEADS * hd), BF16),
        scratch_shapes=[pltpu.VMEM((2, ATT_TQ, LANES), F32),
                        pltpu.VMEM((2, ATT_TQ, LANES), F32),
                        pltpu.VMEM((2, ATT_TQ, hd), F32)],
        compiler_params=_compiler_params(("parallel", "arbitrary")),
        name="diff_attention",
    )(score_bound, qk, qk, v, vec(lq1), vec(lk1), vec(lq2), vec(lk2), vec(subln))


def _out_kernel(a_ref, w_ref, x_ref, g_ref, xo_ref, xn_ref, wb_ref):
    @pl.when(pl.program_id(0) == 0)
    def _():
        wb_ref[...] = w_ref[...].astype(BF16)

    x = x_ref[...] + jnp.dot(a_ref[...], wb_ref[...], preferred_element_type=F32)
    xo_ref[...] = x
    xn_ref[...] = (_rms_rows(x) * g_ref[...]).astype(xn_ref.dtype)


def _out_project(a, w, layer, x, g):
    s, d = x.shape
    k = a.shape[1]
    row = lambda i: (i, 0)
    return pl.pallas_call(
        _out_kernel,
        grid=(s // OUT_TM,),
        in_specs=[pl.BlockSpec((OUT_TM, k), row),
                  pl.BlockSpec((None, k, d), lambda i: (layer, 0, 0),
                               pipeline_mode=pl.Buffered(1)),
                  pl.BlockSpec((OUT_TM, d), row),
                  pl.BlockSpec((1, d), lambda i: (0, 0))],
        out_specs=[pl.BlockSpec((OUT_TM, d), row), pl.BlockSpec((OUT_TM, d), row)],
        out_shape=[jax.ShapeDtypeStruct((s, d), F32), jax.ShapeDtypeStruct((s, d), BF16)],
        scratch_shapes=[pltpu.VMEM((k, d), BF16)],
        compiler_params=_compiler_params(("arbitrary",)),
        name="out_proj_residual_norm",
    )(a, w, x, g.reshape(1, d))


def _ffn_kernel(xn_ref, wg_ref, wu_ref, wd_ref, x_ref, *rest):
    g_ref, xo_ref, xn_out_ref = rest if len(rest) == 3 else (None, rest[0], None)
    f = pl.program_id(1)

    @pl.when(f == 0)
    def _():
        xo_ref[...] = x_ref[...]

    xn = xn_ref[...]
    gate = jnp.dot(xn, wg_ref[...].astype(BF16), preferred_element_type=F32)
    up = jnp.dot(xn, wu_ref[...].astype(BF16), preferred_element_type=F32)
    act = (_silu(gate) * up).astype(BF16)
    xo_ref[...] += jnp.dot(act, wd_ref[...].astype(BF16), preferred_element_type=F32)

    if xn_out_ref is not None:
        @pl.when(f == pl.num_programs(1) - 1)
        def _():
            xn_out_ref[...] = (_rms_rows(xo_ref[...]) * g_ref[...]).astype(xn_out_ref.dtype)


def _ffn(xn, wg, wu, wd, layer, x, g):
    s, d = x.shape
    hidden = wg.shape[2]
    row_block = lambda **kw: pl.BlockSpec((FFN_TM, d), lambda i, f: (i, 0), **kw)
    in_specs = [row_block(),
                pl.BlockSpec((None, d, FFN_TF), lambda i, f: (layer, 0, f)),
                pl.BlockSpec((None, d, FFN_TF), lambda i, f: (layer, 0, f)),
                pl.BlockSpec((None, FFN_TF, d), lambda i, f: (layer, f, 0)),
                row_block()]
    out_specs = [row_block(pipeline_mode=pl.Buffered(1))]
    out_shape = [jax.ShapeDtypeStruct((s, d), F32)]
    args = [xn, wg, wu, wd, x]
    if g is not None:
        in_specs.append(pl.BlockSpec((1, d), lambda i, f: (0, 0)))
        out_specs.append(row_block())
        out_shape.append(jax.ShapeDtypeStruct((s, d), BF16))
        args.append(g.reshape(1, d))
    outs = pl.pallas_call(
        _ffn_kernel,
        grid=(s // FFN_TM, hidden // FFN_TF),
        in_specs=in_specs,
        out_specs=out_specs,
        out_shape=out_shape,
        compiler_params=_compiler_params(("parallel", "arbitrary")),
        name="swiglu_ffn_residual_norm" if g is not None else "swiglu_ffn_residual",
    )(*args)
    return (outs[0], outs[1]) if g is not None else (outs[0], None)


def _rope_tables(seq, dim):
    inv = ROPE_THETA ** (-jnp.arange(0, dim, 2, dtype=F32) / dim)
    ang = jnp.arange(seq, dtype=F32)[:, None] * inv[None, :]
    cos = jnp.cos(ang)
    sin = jnp.sin(ang)
    reps = LANES // dim
    cos_l = jnp.tile(jnp.concatenate([cos, cos], axis=1), (1, reps))
    sin_l = jnp.tile(jnp.concatenate([-sin, sin], axis=1), (1, reps))
    return cos_l, sin_l


def kernel(x, norm_mix, norm_ffn, hyb_w_in, hyb_conv_w, hyb_w_out, diff_w_qkv, diff_q_norm,
           diff_k_norm, diff_lambda_q1, diff_lambda_k1, diff_lambda_q2, diff_lambda_k2,
           diff_subln, diff_w_out, ffn_w_gate, ffn_w_up, ffn_w_down):
    b, s, d = x.shape
    assert b == 1 and d == D_MODEL
    xs = x.reshape(s, d)
    cos_r, sin_r = _rope_tables(s, RET_DK)
    cos_a, sin_a = _rope_tables(s, DIFF_HEAD_DIM)

    xn = _rmsnorm(xs, norm_mix[0])
    for layer in range(DEPTH):
        j = layer // 2
        if layer % 2 == 0:
            qk_cols = 2 * RET_QK
            qk = _project(xn, hyb_w_in, j, 0, qk_cols, rope=(cos_r, sin_r, RET_DK // 2, None))
            rest = _project(xn, hyb_w_in, j, qk_cols, HYB_IN - qk_cols)
            mixed = _hybrid_mix(qk, rest, hyb_conv_w[j])
            w_out = hyb_w_out
        else:
            lambda_init = 0.8 - 0.6 * math.exp(-0.3 * layer)
            n_groups = DIFF_QK_COLS // DIFF_HEAD_DIM
            q_gain = diff_q_norm[j] * (LOG2_E * DIFF_HEAD_DIM ** -0.5)
            gain = jnp.concatenate([jnp.tile(q_gain, n_groups),
                                    jnp.tile(diff_k_norm[j], n_groups)]).reshape(1, -1)
            qk_cols = 2 * DIFF_QK_COLS
            qk = _project(xn, diff_w_qkv, j, 0, qk_cols,
                          rope=(cos_a, sin_a, DIFF_HEAD_DIM // 2, gain))
            v = _project(xn, diff_w_qkv, j, qk_cols, DIFF_QKV - qk_cols)
            score_bound = (ATT_BOUND_SLACK * DIFF_HEAD_DIM * jnp.max(jnp.abs(q_gain))
                           * jnp.max(jnp.abs(diff_k_norm[j]))).reshape(1).astype(F32)
            mixed = _diff_attention(qk, v, score_bound, diff_lambda_q1[j], diff_lambda_k1[j],
                                    diff_lambda_q2[j], diff_lambda_k2[j], diff_subln[j],
                                    lambda_init)
            w_out = diff_w_out
        xs, xn = _out_project(mixed, w_out, j, xs, norm_ffn[layer])
        g_next = norm_mix[layer + 1] if layer + 1 < DEPTH else None
        xs, xn = _ffn(xn, ffn_w_gate, ffn_w_up, ffn_w_down, layer, xs, g_next)
    return xs.reshape(b, s, d)
```

```python
import functools
import math

import jax
import jax.numpy as jnp
import numpy as np
from jax import lax
from jax.experimental import pallas as pl
from jax.experimental.pallas import tpu as pltpu

D_MODEL = 2048
DEPTH = 4
ROPE_THETA = 10000.0
NORM_EPS = 1e-6
RET_HEADS = 8
RET_DK = 64
RET_DV = 128
CONV_WIDTH = 1024
CONV_K = 3
DIFF_HEADS = 8
DIFF_HEAD_DIM = 128
FFN_HIDDEN = 5632
RET_QK = RET_HEADS * RET_DK
RET_V = RET_HEADS * RET_DV
HYB_IN = 2 * RET_QK + 2 * RET_V + 3 * CONV_WIDTH
DIFF_QKV = 6144
DIFF_QK_COLS = 2 * DIFF_HEADS * DIFF_HEAD_DIM

LANES = 128
SUBLANES = 8
VMEM_LIMIT_BYTES = 56 * 1024 * 1024

NORM_TM = 512
PROJ_TM = 2048
PROJ_TN = 512
OUT_TM = 512
FFN_TM = 1024
FFN_TF = 256
RET_TS = 256
ATT_TQ = 2048
ATT_TK = 512
ATT_UNROLL = 2
MASK_VALUE = -1e30
LOG2_E = math.log2(math.e)
ATT_BOUND_SLACK = 1.02
ATT_BOUND_MAX = 60.0

F32 = jnp.float32
BF16 = jnp.bfloat16


def _compiler_params(semantics):
    return pltpu.CompilerParams(dimension_semantics=semantics,
                                vmem_limit_bytes=VMEM_LIMIT_BYTES)


def _rms_rows(v):
    return v * lax.rsqrt(jnp.mean(v * v, axis=-1, keepdims=True) + NORM_EPS)


def _norm_kernel(x_ref, g_ref, o_ref):
    o_ref[...] = (_rms_rows(x_ref[...]) * g_ref[...]).astype(o_ref.dtype)


def _rmsnorm(x, g):
    s, d = x.shape
    return pl.pallas_call(
        _norm_kernel,
        grid=(s // NORM_TM,),
        in_specs=[pl.BlockSpec((NORM_TM, d), lambda i: (i, 0)),
                  pl.BlockSpec((1, d), lambda i: (0, 0))],
        out_specs=pl.BlockSpec((NORM_TM, d), lambda i: (i, 0)),
        out_shape=jax.ShapeDtypeStruct((s, d), BF16),
        compiler_params=_compiler_params(("parallel",)),
        name="rmsnorm",
    )(x, g.reshape(1, d))


def _rotate_half(blk, half):
    if 2 * half == LANES:
        return pltpu.roll(blk, half, axis=1)
    lane = lax.broadcasted_iota(jnp.int32, blk.shape, 1)
    first = (lane % (2 * half)) < half
    return jnp.where(first, pltpu.roll(blk, LANES - half, axis=1),
                     pltpu.roll(blk, half, axis=1))


def _proj_plain_kernel(x_ref, w_ref, o_ref):
    o_ref[...] = jnp.dot(x_ref[...], w_ref[...].astype(BF16),
                         preferred_element_type=F32).astype(o_ref.dtype)


def _proj_rope_kernel(x_ref, w_ref, cos_ref, sin_ref, *rest, rope_half):
    gain_ref, o_ref, acc_even_ref, acc_odd_ref = rest if len(rest) == 4 else (None,) + rest
    t = pl.program_id(0)
    n_groups = acc_even_ref.shape[1] // LANES
    k_chunk = x_ref.shape[1] // n_groups

    @pl.when(t == 0)
    def _():
        acc_odd_ref[...] = jnp.zeros_like(acc_odd_ref)

    def run(cur_ref, prev_ref):
        def chunk_and_group(g, carry):
            ks = pl.ds(pl.multiple_of(g * k_chunk, k_chunk), k_chunk)
            part = jnp.dot(x_ref[:, ks], w_ref[ks, :].astype(BF16),
                           preferred_element_type=F32)
            cur_ref[...] = jnp.where(g == 0, part, cur_ref[...] + part)

            cols = pl.ds(pl.multiple_of(g * LANES, LANES), LANES)
            blk = prev_ref[:, cols]
            if gain_ref is not None:
                blk = _rms_rows(blk) * gain_ref[:, cols]
            out = blk * cos_ref[...] + _rotate_half(blk, rope_half) * sin_ref[...]
            o_ref[:, cols] = out.astype(o_ref.dtype)
            return carry

        lax.fori_loop(0, n_groups, chunk_and_group, 0)

    @pl.when(t % 2 == 0)
    def _():
        run(acc_even_ref, acc_odd_ref)

    @pl.when(t % 2 == 1)
    def _():
        run(acc_odd_ref, acc_even_ref)


def _project_rope(xn, w, layer, n_cols, cos, sin, rope_half, gain):
    s, d = xn.shape
    n_tiles = n_cols // PROJ_TN
    last = (s // PROJ_TM) * n_tiles - 1
    mm_row = lambda t: jnp.minimum(t, last) // n_tiles
    mm_col = lambda t: jnp.minimum(t, last) % n_tiles
    ep_row = lambda t: jnp.maximum(t - 1, 0) // n_tiles
    ep_col = lambda t: jnp.maximum(t - 1, 0) % n_tiles
    in_specs = [pl.BlockSpec((PROJ_TM, d), lambda t: (mm_row(t), 0)),
                pl.BlockSpec((None, d, PROJ_TN), lambda t: (layer, 0, mm_col(t))),
                pl.BlockSpec((PROJ_TM, LANES), lambda t: (ep_row(t), 0)),
                pl.BlockSpec((PROJ_TM, LANES), lambda t: (ep_row(t), 0))]
    args = [xn, w, cos, sin]
    if gain is not None:
        in_specs.append(pl.BlockSpec((1, PROJ_TN), lambda t: (0, ep_col(t))))
        args.append(gain)
    return pl.pallas_call(
        functools.partial(_proj_rope_kernel, rope_half=rope_half),
        grid=(last + 2,),
        in_specs=in_specs,
        out_specs=pl.BlockSpec((PROJ_TM, PROJ_TN), lambda t: (ep_row(t), ep_col(t))),
        out_shape=jax.ShapeDtypeStruct((s, n_cols), BF16),
        scratch_shapes=[pltpu.VMEM((PROJ_TM, PROJ_TN), F32),
                        pltpu.VMEM((PROJ_TM, PROJ_TN), F32)],
        compiler_params=_compiler_params(("arbitrary",)),
        name="proj_rope" if gain is None else "proj_qknorm_rope",
    )(*args)


def _project(xn, w, layer, col_start, n_cols):
    s, d = xn.shape
    first = col_start // PROJ_TN
    return pl.pallas_call(
        _proj_plain_kernel,
        grid=(s // PROJ_TM, n_cols // PROJ_TN),
        in_specs=[pl.BlockSpec((PROJ_TM, d), lambda m, j: (m, 0)),
                  pl.BlockSpec((None, d, PROJ_TN), lambda m, j: (layer, 0, first + j))],
        out_specs=pl.BlockSpec((PROJ_TM, PROJ_TN), lambda m, j: (m, j)),
        out_shape=jax.ShapeDtypeStruct((s, n_cols), BF16),
        compiler_params=_compiler_params(("parallel", "arbitrary")),
        name="proj_plain",
    )(xn, w)


def _retention_tables(ts):
    h = np.arange(RET_HEADS, dtype=np.float64)
    log_g = np.log(1.0 - np.exp2(-5.0 - h))
    idx = np.arange(ts, dtype=np.float64)
    dist = idx[:, None] - idx[None, :]
    scale = RET_DK ** -0.5
    decay = np.where(dist >= 0, np.exp(log_g[:, None, None] * np.maximum(dist, 0.0)), 0.0) * scale
    q_dec = np.exp(log_g[:, None] * (idx + 1.0))
    k_dec = np.exp(log_g[:, None] * (ts - 1.0 - idx)) * scale
    chunk_dec = np.exp(log_g * ts)
    q_dec = np.broadcast_to(q_dec[:, :, None], (RET_HEADS, ts, LANES))
    k_dec = np.broadcast_to(k_dec[:, :, None], (RET_HEADS, ts, LANES))
    return (jnp.asarray(decay, F32), jnp.asarray(q_dec, F32), jnp.asarray(k_dec, F32),
            [float(np.float32(c)) for c in chunk_dec])


def _silu(v):
    return v / (1.0 + jnp.exp(-v))


def _hybrid_kernel(q_ref, k_ref, v_ref, g_ref, cb_ref, cc_ref, cx_ref, convw_ref,
                   decay_ref, qdec_ref, kdec_ref, o_ref, state_ref, u_ref, *, chunk_dec):
    ts = q_ref.shape[0]

    @pl.when(pl.program_id(0) == 0)
    def _():
        state_ref[...] = jnp.zeros_like(state_ref)
        u_ref[0:SUBLANES, :] = jnp.zeros((SUBLANES, u_ref.shape[1]), F32)

    lane = lax.broadcasted_iota(jnp.int32, (ts, LANES), 1)
    for h in range(RET_HEADS):
        pair = slice((h // 2) * LANES, (h // 2 + 1) * LANES)
        head = slice(h * RET_DV, (h + 1) * RET_DV)
        in_head = (lane // RET_DK) == (h % 2)
        qm = jnp.where(in_head, q_ref[:, pair].astype(F32), 0.0)
        kp = k_ref[:, pair]
        vh = v_ref[:, head]
        s = lax.dot_general(qm.astype(BF16), kp, (((1,), (1,)), ((), ())),
                            preferred_element_type=F32)
        inner = (s * decay_ref[h]).astype(BF16)
        qd = (qm * qdec_ref[h]).astype(BF16)
        o = (jnp.dot(inner, vh, preferred_element_type=F32)
             + jnp.dot(qd, state_ref[h].astype(BF16), preferred_element_type=F32))
        kd = (kp.astype(F32) * kdec_ref[h]).astype(BF16)
        state_ref[h] = chunk_dec[h] * state_ref[h] + lax.dot_general(
            kd, vh, (((0,), (0,)), ((), ())), preferred_element_type=F32)
        gate = _silu(g_ref[:, head].astype(F32))
        o_ref[:, head] = (gate * _rms_rows(o)).astype(o_ref.dtype)

    u_ref[SUBLANES:SUBLANES + ts, :] = cc_ref[...].astype(F32) * cx_ref[...].astype(F32)
    y = (convw_ref[0:1, :] * u_ref[SUBLANES - 2:SUBLANES - 2 + ts, :]
         + convw_ref[1:2, :] * u_ref[SUBLANES - 1:SUBLANES - 1 + ts, :]
         + convw_ref[2:3, :] * u_ref[SUBLANES:SUBLANES + ts, :])
    o_ref[:, RET_V:] = (cb_ref[...].astype(F32) * y).astype(o_ref.dtype)
    u_ref[0:SUBLANES, :] = u_ref[ts:ts + SUBLANES, :]


def _hybrid_mix(qk, rest, conv_w):
    s = qk.shape[0]
    ts = RET_TS
    decay, q_dec, k_dec, chunk_dec = _retention_tables(ts)
    qk_blk = RET_QK
    w_blk = RET_V
    const3 = lambda i: (0, 0, 0)
    kernel = functools.partial(_hybrid_kernel, chunk_dec=chunk_dec)
    return pl.pallas_call(
        kernel,
        grid=(s // ts,),
        in_specs=[pl.BlockSpec((ts, qk_blk), lambda i: (i, 0)),
                  pl.BlockSpec((ts, qk_blk), lambda i: (i, 1)),
                  pl.BlockSpec((ts, w_blk), lambda i: (i, 0)),
                  pl.BlockSpec((ts, w_blk), lambda i: (i, 1)),
                  pl.BlockSpec((ts, w_blk), lambda i: (i, 2)),
                  pl.BlockSpec((ts, w_blk), lambda i: (i, 3)),
                  pl.BlockSpec((ts, w_blk), lambda i: (i, 4)),
                  pl.BlockSpec((SUBLANES, CONV_WIDTH), lambda i: (0, 0)),
                  pl.BlockSpec((RET_HEADS, ts, ts), const3),
                  pl.BlockSpec((RET_HEADS, ts, LANES), const3),
                  pl.BlockSpec((RET_HEADS, ts, LANES), const3)],
        out_specs=pl.BlockSpec((ts, RET_V + CONV_WIDTH), lambda i: (i, 0)),
        out_shape=jax.ShapeDtypeStruct((s, RET_V + CONV_WIDTH), BF16),
        scratch_shapes=[pltpu.VMEM((RET_HEADS, LANES, RET_DV), F32),
                        pltpu.VMEM((SUBLANES + ts, CONV_WIDTH), F32)],
        compiler_params=_compiler_params(("arbitrary",)),
        name="retention_conv",
    )(qk, qk, rest, rest, rest, rest, rest,
      jnp.pad(conv_w, ((0, SUBLANES - CONV_K), (0, 0))), decay, q_dec, k_dec)


def _lane_partial_sum(p):
    total = p[:, 0:LANES]
    for t in range(1, p.shape[1] // LANES):
        total = total + p[:, t * LANES:(t + 1) * LANES]
    return total


def _attn_kernel(bound_ref, q_ref, k_ref, v_ref, lq1_ref, lk1_ref, lq2_ref, lk2_ref,
                 subln_ref, o_ref, m_ref, l_ref, acc_ref, *, lambda_init):
    tq = q_ref.shape[0]
    tk = ATT_TK
    d = DIFF_HEAD_DIM
    qi = pl.program_id(1)
    bound = bound_ref[0]

    l_ref[...] = jnp.zeros_like(l_ref)
    acc_ref[...] = jnp.zeros_like(acc_ref)

    def scores(j, i, rows, masked):
        start = pl.multiple_of(j * tk, tk)
        s = lax.dot_general(q_ref[rows, i * d:(i + 1) * d],
                            k_ref[pl.ds(start, tk), i * d:(i + 1) * d],
                            (((1,), (1,)), ((), ())), preferred_element_type=F32)
        if masked:
            row = lax.broadcasted_iota(jnp.int32, s.shape, 0)
            col = lax.broadcasted_iota(jnp.int32, s.shape, 1)
            s = jnp.where(col <= row, s, MASK_VALUE)
        return s, v_ref[pl.ds(start, tk), :]

    def update_bounded(j, rows, masked):
        for i in range(2):
            s, v = scores(j, i, rows, masked)
            p = jnp.exp2(s - bound)
            l_ref[i, rows, :] += _lane_partial_sum(p)
            acc_ref[i, rows, :] += jnp.dot(p.astype(BF16), v, preferred_element_type=F32)

    def update_online(j, rows, masked):
        for i in range(2):
            s, v = scores(j, i, rows, masked)
            m_prev = m_ref[i, rows, :]
            m_new = jnp.maximum(m_prev, jnp.max(s, axis=1, keepdims=True))
            alpha = jnp.exp2(m_prev - m_new)
            p = jnp.exp2(s - m_new[:, :1])
            l_ref[i, rows, :] = alpha * l_ref[i, rows, :] + _lane_partial_sum(p)
            acc_ref[i, rows, :] = alpha[:, :1] * acc_ref[i, rows, :] + jnp.dot(
                p.astype(BF16), v, preferred_element_type=F32)
            m_ref[i, rows, :] = m_new

    def run(update):
        blocks_per_step = tq // tk
        all_rows = slice(0, tq)

        def body(t, carry):
            for u in range(ATT_UNROLL):
                update(ATT_UNROLL * t + u, all_rows, masked=False)
            return carry
        lax.fori_loop(0, (blocks_per_step // ATT_UNROLL) * qi, body, 0)
        for r in range(blocks_per_step):
            update(blocks_per_step * qi + r, slice(r * tk, tq), masked=True)

    @pl.when(bound <= ATT_BOUND_MAX)
    def _():
        run(update_bounded)

    @pl.when(bound > ATT_BOUND_MAX)
    def _():
        m_ref[...] = jnp.full_like(m_ref, MASK_VALUE)
        run(update_online)

    lam = (jnp.exp(jnp.sum(lq1_ref[...] * lk1_ref[...], axis=1, keepdims=True))
           - jnp.exp(jnp.sum(lq2_ref[...] * lk2_ref[...], axis=1, keepdims=True))
           + lambda_init)
    l0 = jnp.sum(l_ref[0], axis=1, keepdims=True)
    l1 = jnp.sum(l_ref[1], axis=1, keepdims=True)
    o = acc_ref[0] / l0 - lam * (acc_ref[1] / l1)
    o = _rms_rows(o) * subln_ref[...] * (1.0 - lambda_init)
    o_ref[...] = o.astype(o_ref.dtype)


def _diff_attention(qk, v, score_bound, lq1, lk1, lq2, lk2, subln, lambda_init):
    s = qk.shape[0]
    hd = 2 * DIFF_HEAD_DIM
    assert ATT_TQ % ATT_TK == 0
    vec = lambda a: a.reshape(1, -1).astype(F32)
    small = lambda w: pl.BlockSpec((1, w), lambda h, i: (0, 0))
    kernel = functools.partial(_attn_kernel, lambda_init=lambda_init)
    return pl.pallas_call(
        kernel,
        grid=(DIFF_HEADS, s // ATT_TQ),
        in_specs=[pl.BlockSpec(memory_space=pltpu.SMEM),
                  pl.BlockSpec((ATT_TQ, hd), lambda h, i: (i, h)),
                  pl.BlockSpec((s, hd), lambda h, i: (0, DIFF_HEADS + h)),
                  pl.BlockSpec((s, hd), lambda h, i: (0, h)),
                  small(DIFF_HEAD_DIM), small(DIFF_HEAD_DIM),
                  small(DIFF_HEAD_DIM), small(DIFF_HEAD_DIM), small(hd)],
        out_specs=pl.BlockSpec((ATT_TQ, hd), lambda h, i: (i, h)),
        out_shape=jax.ShapeDtypeStruct((s, DIFF_HEADS * hd), BF16),
        scratch_shapes=[pltpu.VMEM((2, ATT_TQ, LANES), F32),
                        pltpu.VMEM((2, ATT_TQ, LANES), F32),
                        pltpu.VMEM((2, ATT_TQ, hd), F32)],
        compiler_params=_compiler_params(("parallel", "arbitrary")),
        name="diff_attention",
    )(score_bound, qk, qk, v, vec(lq1), vec(lk1), vec(lq2), vec(lk2), vec(subln))


def _out_kernel(a_ref, w_ref, x_ref, g_ref, xo_ref, xn_ref, wb_ref):
    @pl.when(pl.program_id(0) == 0)
    def _():
        wb_ref[...] = w_ref[...].astype(BF16)

    x = x_ref[...] + jnp.dot(a_ref[...], wb_ref[...], preferred_element_type=F32)
    xo_ref[...] = x
    xn_ref[...] = (_rms_rows(x) * g_ref[...]).astype(xn_ref.dtype)


def _out_project(a, w, layer, x, g):
    s, d = x.shape
    k = a.shape[1]
    row = lambda i: (i, 0)
    return pl.pallas_call(
        _out_kernel,
        grid=(s // OUT_TM,),
        in_specs=[pl.BlockSpec((OUT_TM, k), row),
                  pl.BlockSpec((None, k, d), lambda i: (layer, 0, 0),
                               pipeline_mode=pl.Buffered(1)),
                  pl.BlockSpec((OUT_TM, d), row),
                  pl.BlockSpec((1, d), lambda i: (0, 0))],
        out_specs=[pl.BlockSpec((OUT_TM, d), row), pl.BlockSpec((OUT_TM, d), row)],
        out_shape=[jax.ShapeDtypeStruct((s, d), F32), jax.ShapeDtypeStruct((s, d), BF16)],
        scratch_shapes=[pltpu.VMEM((k, d), BF16)],
        compiler_params=_compiler_params(("arbitrary",)),
        name="out_proj_residual_norm",
    )(a, w, x, g.reshape(1, d))


def _ffn_kernel(xn_ref, wg_ref, wu_ref, wd_ref, x_ref, *rest):
    g_ref, xo_ref, xn_out_ref = rest if len(rest) == 3 else (None, rest[0], None)
    f = pl.program_id(1)

    @pl.when(f == 0)
    def _():
        xo_ref[...] = x_ref[...]

    xn = xn_ref[...]
    gate = jnp.dot(xn, wg_ref[...].astype(BF16), preferred_element_type=F32)
    up = jnp.dot(xn, wu_ref[...].astype(BF16), preferred_element_type=F32)
    act = (_silu(gate) * up).astype(BF16)
    xo_ref[...] += jnp.dot(act, wd_ref[...].astype(BF16), preferred_element_type=F32)

    if xn_out_ref is not None:
        @pl.when(f == pl.num_programs(1) - 1)
        def _():
            xn_out_ref[...] = (_rms_rows(xo_ref[...]) * g_ref[...]).astype(xn_out_ref.dtype)


def _ffn(xn, wg, wu, wd, layer, x, g):
    s, d = x.shape
    hidden = wg.shape[2]
    row_block = lambda **kw: pl.BlockSpec((FFN_TM, d), lambda i, f: (i, 0), **kw)
    in_specs = [row_block(),
                pl.BlockSpec((None, d, FFN_TF), lambda i, f: (layer, 0, f)),
                pl.BlockSpec((None, d, FFN_TF), lambda i, f: (layer, 0, f)),
                pl.BlockSpec((None, FFN_TF, d), lambda i, f: (layer, f, 0)),
                row_block()]
    out_specs = [row_block(pipeline_mode=pl.Buffered(1))]
    out_shape = [jax.ShapeDtypeStruct((s, d), F32)]
    args = [xn, wg, wu, wd, x]
    if g is not None:
        in_specs.append(pl.BlockSpec((1, d), lambda i, f: (0, 0)))
        out_specs.append(row_block())
        out_shape.append(jax.ShapeDtypeStruct((s, d), BF16))
        args.append(g.reshape(1, d))
    outs = pl.pallas_call(
        _ffn_kernel,
        grid=(s // FFN_TM, hidden // FFN_TF),
        in_specs=in_specs,
        out_specs=out_specs,
        out_shape=out_shape,
        compiler_params=_compiler_params(("parallel", "arbitrary")),
        name="swiglu_ffn_residual_norm" if g is not None else "swiglu_ffn_residual",
    )(*args)
    return (outs[0], outs[1]) if g is not None else (outs[0], None)


def _rope_tables(seq, dim):
    inv = ROPE_THETA ** (-jnp.arange(0, dim, 2, dtype=F32) / dim)
    ang = jnp.arange(seq, dtype=F32)[:, None] * inv[None, :]
    cos = jnp.cos(ang)
    sin = jnp.sin(ang)
    reps = LANES // dim
    cos_l = jnp.tile(jnp.concatenate([cos, cos], axis=1), (1, reps))
    sin_l = jnp.tile(jnp.concatenate([-sin, sin], axis=1), (1, reps))
    return cos_l, sin_l


def kernel(x, norm_mix, norm_ffn, hyb_w_in, hyb_conv_w, hyb_w_out, diff_w_qkv, diff_q_norm,
           diff_k_norm, diff_lambda_q1, diff_lambda_k1, diff_lambda_q2, diff_lambda_k2,
           diff_subln, diff_w_out, ffn_w_gate, ffn_w_up, ffn_w_down):
    b, s, d = x.shape
    assert b == 1 and d == D_MODEL
    xs = x.reshape(s, d)
    cos_r, sin_r = _rope_tables(s, RET_DK)
    cos_a, sin_a = _rope_tables(s, DIFF_HEAD_DIM)

    xn = _rmsnorm(xs, norm_mix[0])
    for layer in range(DEPTH):
        j = layer // 2
        if layer % 2 == 0:
            qk_cols = 2 * RET_QK
            qk = _project_rope(xn, hyb_w_in, j, qk_cols, cos_r, sin_r, RET_DK // 2, None)
            rest = _project(xn, hyb_w_in, j, qk_cols, HYB_IN - qk_cols)
            mixed = _hybrid_mix(qk, rest, hyb_conv_w[j])
            w_out = hyb_w_out
        else:
            lambda_init = 0.8 - 0.6 * math.exp(-0.3 * layer)
            n_groups = DIFF_QK_COLS // DIFF_HEAD_DIM
            q_gain = diff_q_norm[j] * (LOG2_E * DIFF_HEAD_DIM ** -0.5)
            gain = jnp.concatenate([jnp.tile(q_gain, n_groups),
                                    jnp.tile(diff_k_norm[j], n_groups)]).reshape(1, -1)
            qk_cols = 2 * DIFF_QK_COLS
            qk = _project_rope(xn, diff_w_qkv, j, qk_cols, cos_a, sin_a,
                               DIFF_HEAD_DIM // 2, gain)
            v = _project(xn, diff_w_qkv, j, qk_cols, DIFF_QKV - qk_cols)
            score_bound = (ATT_BOUND_SLACK * DIFF_HEAD_DIM * jnp.max(jnp.abs(q_gain))
                           * jnp.max(jnp.abs(diff_k_norm[j]))).reshape(1).astype(F32)
            mixed = _diff_attention(qk, v, score_bound, diff_lambda_q1[j], diff_lambda_k1[j],
                                    diff_lambda_q2[j], diff_lambda_k2[j], diff_subln[j],
                                    lambda_init)
            w_out = diff_w_out
        xs, xn = _out_project(mixed, w_out, j, xs, norm_ffn[layer])
        g_next = norm_mix[layer + 1] if layer + 1 < DEPTH else None
        xs, xn = _ffn(xn, ffn_w_gate, ffn_w_up, ffn_w_down, layer, xs, g_next)
    return xs.reshape(b, s, d)
```

```python
import functools
import math

import jax
import jax.numpy as jnp
import numpy as np
from jax import lax
from jax.experimental import pallas as pl
from jax.experimental.pallas import tpu as pltpu

D_MODEL = 2048
DEPTH = 4
ROPE_THETA = 10000.0
NORM_EPS = 1e-6
RET_HEADS = 8
RET_DK = 64
RET_DV = 128
CONV_WIDTH = 1024
CONV_K = 3
DIFF_HEADS = 8
DIFF_HEAD_DIM = 128
FFN_HIDDEN = 5632
RET_QK = RET_HEADS * RET_DK
RET_V = RET_HEADS * RET_DV
HYB_IN = 2 * RET_QK + 2 * RET_V + 3 * CONV_WIDTH
DIFF_QKV = 6144
DIFF_QK_COLS = 2 * DIFF_HEADS * DIFF_HEAD_DIM

LANES = 128
SUBLANES = 8
VMEM_LIMIT_BYTES = 62 * 1024 * 1024

NORM_TM = 512
PROJ_TM = 2048
PROJ_TN = 512
OUT_TM = 512
FFN_TM = 1024
FFN_TF = 256
RET_TS = 256
ATT_TQ = 2048
ATT_TK = 512
ATT_UNROLL = 2
MASK_VALUE = -1e30
LOG2_E = math.log2(math.e)
ATT_BOUND_SLACK = 1.02
ATT_BOUND_MAX = 60.0

F32 = jnp.float32
BF16 = jnp.bfloat16


def _compiler_params(semantics):
    return pltpu.CompilerParams(dimension_semantics=semantics,
                                vmem_limit_bytes=VMEM_LIMIT_BYTES)


def _rms_rows(v):
    return v * lax.rsqrt(jnp.mean(v * v, axis=-1, keepdims=True) + NORM_EPS)


def _norm_kernel(x_ref, g_ref, o_ref):
    o_ref[...] = (_rms_rows(x_ref[...]) * g_ref[...]).astype(o_ref.dtype)


def _rmsnorm(x, g):
    s, d = x.shape
    return pl.pallas_call(
        _norm_kernel,
        grid=(s // NORM_TM,),
        in_specs=[pl.BlockSpec((NORM_TM, d), lambda i: (i, 0)),
                  pl.BlockSpec((1, d), lambda i: (0, 0))],
        out_specs=pl.BlockSpec((NORM_TM, d), lambda i: (i, 0)),
        out_shape=jax.ShapeDtypeStruct((s, d), BF16),
        compiler_params=_compiler_params(("parallel",)),
        name="rmsnorm",
    )(x, g.reshape(1, d))


def _rotate_half(blk, half):
    if 2 * half == LANES:
        return pltpu.roll(blk, half, axis=1)
    lane = lax.broadcasted_iota(jnp.int32, blk.shape, 1)
    first = (lane % (2 * half)) < half
    return jnp.where(first, pltpu.roll(blk, LANES - half, axis=1),
                     pltpu.roll(blk, half, axis=1))


def _proj_plain_kernel(x_ref, w_ref, o_ref):
    o_ref[...] = jnp.dot(x_ref[...], w_ref[...].astype(BF16),
                         preferred_element_type=F32).astype(o_ref.dtype)


def _proj_rope_kernel(x_ref, w_ref, cos_ref, sin_ref, *rest, rope_half):
    gain_ref, o_ref, acc_even_ref, acc_odd_ref = rest if len(rest) == 4 else (None,) + rest
    t = pl.program_id(0)
    n_groups = acc_even_ref.shape[1] // LANES
    k_chunk = x_ref.shape[1] // n_groups

    @pl.when(t == 0)
    def _():
        acc_odd_ref[...] = jnp.zeros_like(acc_odd_ref)

    def run(cur_ref, prev_ref):
        cur_ref[...] = jnp.dot(x_ref[...], w_ref[...].astype(BF16),
                               preferred_element_type=F32)
        for g in range(n_groups):
            cols = slice(g * LANES, (g + 1) * LANES)
            blk = prev_ref[:, cols]
            if gain_ref is not None:
                blk = _rms_rows(blk) * gain_ref[:, cols]
            out = blk * cos_ref[...] + _rotate_half(blk, rope_half) * sin_ref[...]
            o_ref[:, cols] = out.astype(o_ref.dtype)

    @pl.when(t % 2 == 0)
    def _():
        run(acc_even_ref, acc_odd_ref)

    @pl.when(t % 2 == 1)
    def _():
        run(acc_odd_ref, acc_even_ref)


def _project_rope(xn, w, layer, n_cols, cos, sin, rope_half, gain):
    s, d = xn.shape
    n_tiles = n_cols // PROJ_TN
    last = (s // PROJ_TM) * n_tiles - 1
    mm_row = lambda t: jnp.minimum(t, last) // n_tiles
    mm_col = lambda t: jnp.minimum(t, last) % n_tiles
    ep_row = lambda t: jnp.maximum(t - 1, 0) // n_tiles
    ep_col = lambda t: jnp.maximum(t - 1, 0) % n_tiles
    in_specs = [pl.BlockSpec((PROJ_TM, d), lambda t: (mm_row(t), 0)),
                pl.BlockSpec((None, d, PROJ_TN), lambda t: (layer, 0, mm_col(t))),
                pl.BlockSpec((PROJ_TM, LANES), lambda t: (ep_row(t), 0)),
                pl.BlockSpec((PROJ_TM, LANES), lambda t: (ep_row(t), 0))]
    args = [xn, w, cos, sin]
    if gain is not None:
        in_specs.append(pl.BlockSpec((1, PROJ_TN), lambda t: (0, ep_col(t))))
        args.append(gain)
    return pl.pallas_call(
        functools.partial(_proj_rope_kernel, rope_half=rope_half),
        grid=(last + 2,),
        in_specs=in_specs,
        out_specs=pl.BlockSpec((PROJ_TM, PROJ_TN), lambda t: (ep_row(t), ep_col(t))),
        out_shape=jax.ShapeDtypeStruct((s, n_cols), BF16),
        scratch_shapes=[pltpu.VMEM((PROJ_TM, PROJ_TN), F32),
                        pltpu.VMEM((PROJ_TM, PROJ_TN), F32)],
        compiler_params=_compiler_params(("arbitrary",)),
        name="proj_rope" if gain is None else "proj_qknorm_rope",
    )(*args)


def _project(xn, w, layer, col_start, n_cols):
    s, d = xn.shape
    first = col_start // PROJ_TN
    return pl.pallas_call(
        _proj_plain_kernel,
        grid=(s // PROJ_TM, n_cols // PROJ_TN),
        in_specs=[pl.BlockSpec((PROJ_TM, d), lambda m, j: (m, 0)),
                  pl.BlockSpec((None, d, PROJ_TN), lambda m, j: (layer, 0, first + j))],
        out_specs=pl.BlockSpec((PROJ_TM, PROJ_TN), lambda m, j: (m, j)),
        out_shape=jax.ShapeDtypeStruct((s, n_cols), BF16),
        compiler_params=_compiler_params(("parallel", "arbitrary")),
        name="proj_plain",
    )(xn, w)


def _retention_tables(ts):
    h = np.arange(RET_HEADS, dtype=np.float64)
    log_g = np.log(1.0 - np.exp2(-5.0 - h))
    idx = np.arange(ts, dtype=np.float64)
    dist = idx[:, None] - idx[None, :]
    scale = RET_DK ** -0.5
    decay = np.where(dist >= 0, np.exp(log_g[:, None, None] * np.maximum(dist, 0.0)), 0.0) * scale
    q_dec = np.exp(log_g[:, None] * (idx + 1.0))
    k_dec = np.exp(log_g[:, None] * (ts - 1.0 - idx)) * scale
    chunk_dec = np.exp(log_g * ts)
    q_dec = np.broadcast_to(q_dec[:, :, None], (RET_HEADS, ts, LANES))
    k_dec = np.broadcast_to(k_dec[:, :, None], (RET_HEADS, ts, LANES))
    return (jnp.asarray(decay, F32), jnp.asarray(q_dec, F32), jnp.asarray(k_dec, F32),
            [float(np.float32(c)) for c in chunk_dec])


def _silu(v):
    return v / (1.0 + jnp.exp(-v))


def _hybrid_kernel(q_ref, k_ref, v_ref, g_ref, cb_ref, cc_ref, cx_ref, convw_ref,
                   decay_ref, qdec_ref, kdec_ref, o_ref, state_ref, u_ref, *, chunk_dec):
    ts = q_ref.shape[0]

    @pl.when(pl.program_id(0) == 0)
    def _():
        state_ref[...] = jnp.zeros_like(state_ref)
        u_ref[0:SUBLANES, :] = jnp.zeros((SUBLANES, u_ref.shape[1]), F32)

    lane = lax.broadcasted_iota(jnp.int32, (ts, LANES), 1)
    for h in range(RET_HEADS):
        pair = slice((h // 2) * LANES, (h // 2 + 1) * LANES)
        head = slice(h * RET_DV, (h + 1) * RET_DV)
        in_head = (lane // RET_DK) == (h % 2)
        qm = jnp.where(in_head, q_ref[:, pair].astype(F32), 0.0)
        kp = k_ref[:, pair]
        vh = v_ref[:, head]
        s = lax.dot_general(qm.astype(BF16), kp, (((1,), (1,)), ((), ())),
                            preferred_element_type=F32)
        inner = (s * decay_ref[h]).astype(BF16)
        qd = (qm * qdec_ref[h]).astype(BF16)
        o = (jnp.dot(inner, vh, preferred_element_type=F32)
             + jnp.dot(qd, state_ref[h].astype(BF16), preferred_element_type=F32))
        kd = (kp.astype(F32) * kdec_ref[h]).astype(BF16)
        state_ref[h] = chunk_dec[h] * state_ref[h] + lax.dot_general(
            kd, vh, (((0,), (0,)), ((), ())), preferred_element_type=F32)
        gate = _silu(g_ref[:, head].astype(F32))
        o_ref[:, head] = (gate * _rms_rows(o)).astype(o_ref.dtype)

    u_ref[SUBLANES:SUBLANES + ts, :] = cc_ref[...].astype(F32) * cx_ref[...].astype(F32)
    y = (convw_ref[0:1, :] * u_ref[SUBLANES - 2:SUBLANES - 2 + ts, :]
         + convw_ref[1:2, :] * u_ref[SUBLANES - 1:SUBLANES - 1 + ts, :]
         + convw_ref[2:3, :] * u_ref[SUBLANES:SUBLANES + ts, :])
    o_ref[:, RET_V:] = (cb_ref[...].astype(F32) * y).astype(o_ref.dtype)
    u_ref[0:SUBLANES, :] = u_ref[ts:ts + SUBLANES, :]


def _hybrid_mix(qk, rest, conv_w):
    s = qk.shape[0]
    ts = RET_TS
    decay, q_dec, k_dec, chunk_dec = _retention_tables(ts)
    qk_blk = RET_QK
    w_blk = RET_V
    const3 = lambda i: (0, 0, 0)
    kernel = functools.partial(_hybrid_kernel, chunk_dec=chunk_dec)
    return pl.pallas_call(
        kernel,
        grid=(s // ts,),
        in_specs=[pl.BlockSpec((ts, qk_blk), lambda i: (i, 0)),
                  pl.BlockSpec((ts, qk_blk), lambda i: (i, 1)),
                  pl.BlockSpec((ts, w_blk), lambda i: (i, 0)),
                  pl.BlockSpec((ts, w_blk), lambda i: (i, 1)),
                  pl.BlockSpec((ts, w_blk), lambda i: (i, 2)),
                  pl.BlockSpec((ts, w_blk), lambda i: (i, 3)),
                  pl.BlockSpec((ts, w_blk), lambda i: (i, 4)),
                  pl.BlockSpec((SUBLANES, CONV_WIDTH), lambda i: (0, 0)),
                  pl.BlockSpec((RET_HEADS, ts, ts), const3),
                  pl.BlockSpec((RET_HEADS, ts, LANES), const3),
                  pl.BlockSpec((RET_HEADS, ts, LANES), const3)],
        out_specs=pl.BlockSpec((ts, RET_V + CONV_WIDTH), lambda i: (i, 0)),
        out_shape=jax.ShapeDtypeStruct((s, RET_V + CONV_WIDTH), BF16),
        scratch_shapes=[pltpu.VMEM((RET_HEADS, LANES, RET_DV), F32),
                        pltpu.VMEM((SUBLANES + ts, CONV_WIDTH), F32)],
        compiler_params=_compiler_params(("arbitrary",)),
        name="retention_conv",
    )(qk, qk, rest, rest, rest, rest, rest,
      jnp.pad(conv_w, ((0, SUBLANES - CONV_K), (0, 0))), decay, q_dec, k_dec)


def _lane_partial_sum(p):
    total = p[:, 0:LANES]
    for t in range(1, p.shape[1] // LANES):
        total = total + p[:, t * LANES:(t + 1) * LANES]
    return total


def _attn_kernel(bound_ref, q_ref, k_ref, v_ref, lq1_ref, lk1_ref, lq2_ref, lk2_ref,
                 subln_ref, o_ref, m_ref, l_ref, acc_ref, *, lambda_init):
    tq = q_ref.shape[0]
    tk = ATT_TK
    d = DIFF_HEAD_DIM
    qi = pl.program_id(1)
    bound = bound_ref[0]

    l_ref[...] = jnp.zeros_like(l_ref)
    acc_ref[...] = jnp.zeros_like(acc_ref)

    def scores(j, i, rows, masked):
        start = pl.multiple_of(j * tk, tk)
        s = lax.dot_general(q_ref[rows, i * d:(i + 1) * d],
                            k_ref[pl.ds(start, tk), i * d:(i + 1) * d],
                            (((1,), (1,)), ((), ())), preferred_element_type=F32)
        if masked:
            row = lax.broadcasted_iota(jnp.int32, s.shape, 0)
            col = lax.broadcasted_iota(jnp.int32, s.shape, 1)
            s = jnp.where(col <= row, s, MASK_VALUE)
        return s, v_ref[pl.ds(start, tk), :]

    def update_bounded(j, rows, masked):
        for i in range(2):
            s, v = scores(j, i, rows, masked)
            p = jnp.exp2(s - bound)
            l_ref[i, rows, :] += _lane_partial_sum(p)
            acc_ref[i, rows, :] += jnp.dot(p.astype(BF16), v, preferred_element_type=F32)

    def update_online(j, rows, masked):
        for i in range(2):
            s, v = scores(j, i, rows, masked)
            m_prev = m_ref[i, rows, :]
            m_new = jnp.maximum(m_prev, jnp.max(s, axis=1, keepdims=True))
            alpha = jnp.exp2(m_prev - m_new)
            p = jnp.exp2(s - m_new[:, :1])
            l_ref[i, rows, :] = alpha * l_ref[i, rows, :] + _lane_partial_sum(p)
            acc_ref[i, rows, :] = alpha[:, :1] * acc_ref[i, rows, :] + jnp.dot(
                p.astype(BF16), v, preferred_element_type=F32)
            m_ref[i, rows, :] = m_new

    def run(update):
        blocks_per_step = tq // tk
        all_rows = slice(0, tq)

        def body(t, carry):
            for u in range(ATT_UNROLL):
                update(ATT_UNROLL * t + u, all_rows, masked=False)
            return carry
        lax.fori_loop(0, (blocks_per_step // ATT_UNROLL) * qi, body, 0)
        for r in range(blocks_per_step):
            update(blocks_per_step * qi + r, slice(r * tk, tq), masked=True)

    @pl.when(bound <= ATT_BOUND_MAX)
    def _():
        run(update_bounded)

    @pl.when(bound > ATT_BOUND_MAX)
    def _():
        m_ref[...] = jnp.full_like(m_ref, MASK_VALUE)
        run(update_online)

    lam = (jnp.exp(jnp.sum(lq1_ref[...] * lk1_ref[...], axis=1, keepdims=True))
           - jnp.exp(jnp.sum(lq2_ref[...] * lk2_ref[...], axis=1, keepdims=True))
           + lambda_init)
    l0 = jnp.sum(l_ref[0], axis=1, keepdims=True)
    l1 = jnp.sum(l_ref[1], axis=1, keepdims=True)
    o = acc_ref[0] / l0 - lam * (acc_ref[1] / l1)
    o = _rms_rows(o) * subln_ref[...] * (1.0 - lambda_init)
    o_ref[...] = o.astype(o_ref.dtype)


def _diff_attention(qk, v, score_bound, lq1, lk1, lq2, lk2, subln, lambda_init):
    s = qk.shape[0]
    hd = 2 * DIFF_HEAD_DIM
    assert ATT_TQ % ATT_TK == 0
    vec = lambda a: a.reshape(1, -1).astype(F32)
    small = lambda w: pl.BlockSpec((1, w), lambda h, i: (0, 0))
    kernel = functools.partial(_attn_kernel, lambda_init=lambda_init)
    return pl.pallas_call(
        kernel,
        grid=(DIFF_HEADS, s // ATT_TQ),
        in_specs=[pl.BlockSpec(memory_space=pltpu.SMEM),
                  pl.BlockSpec((ATT_TQ, hd), lambda h, i: (i, h)),
                  pl.BlockSpec((s, hd), lambda h, i: (0, DIFF_HEADS + h)),
                  pl.BlockSpec((s, hd), lambda h, i: (0, h)),
                  small(DIFF_HEAD_DIM), small(DIFF_HEAD_DIM),
                  small(DIFF_HEAD_DIM), small(DIFF_HEAD_DIM), small(hd)],
        out_specs=pl.BlockSpec((ATT_TQ, hd), lambda h, i: (i, h)),
        out_shape=jax.ShapeDtypeStruct((s, DIFF_HEADS * hd), BF16),
        scratch_shapes=[pltpu.VMEM((2, ATT_TQ, LANES), F32),
                        pltpu.VMEM((2, ATT_TQ, LANES), F32),
                        pltpu.VMEM((2, ATT_TQ, hd), F32)],
        compiler_params=_compiler_params(("parallel", "arbitrary")),
        name="diff_attention",
    )(score_bound, qk, qk, v, vec(lq1), vec(lk1), vec(lq2), vec(lk2), vec(subln))


def _out_kernel(a_ref, w_ref, x_ref, g_ref, xo_ref, xn_ref, wb_ref):
    @pl.when(pl.program_id(0) == 0)
    def _():
        wb_ref[...] = w_ref[...].astype(BF16)

    x = x_ref[...] + jnp.dot(a_ref[...], wb_ref[...], preferred_element_type=F32)
    xo_ref[...] = x
    xn_ref[...] = (_rms_rows(x) * g_ref[...]).astype(xn_ref.dtype)


def _out_project(a, w, layer, x, g):
    s, d = x.shape
    k = a.shape[1]
    row = lambda i: (i, 0)
    return pl.pallas_call(
        _out_kernel,
        grid=(s // OUT_TM,),
        in_specs=[pl.BlockSpec((OUT_TM, k), row),
                  pl.BlockSpec((None, k, d), lambda i: (layer, 0, 0),
                               pipeline_mode=pl.Buffered(1)),
                  pl.BlockSpec((OUT_TM, d), row),
                  pl.BlockSpec((1, d), lambda i: (0, 0))],
        out_specs=[pl.BlockSpec((OUT_TM, d), row), pl.BlockSpec((OUT_TM, d), row)],
        out_shape=[jax.ShapeDtypeStruct((s, d), F32), jax.ShapeDtypeStruct((s, d), BF16)],
        scratch_shapes=[pltpu.VMEM((k, d), BF16)],
        compiler_params=_compiler_params(("arbitrary",)),
        name="out_proj_residual_norm",
    )(a, w, x, g.reshape(1, d))


def _ffn_kernel(xn_ref, wg_ref, wu_ref, wd_ref, x_ref, *rest):
    g_ref, xo_ref, xn_out_ref = rest if len(rest) == 3 else (None, rest[0], None)
    f = pl.program_id(1)

    @pl.when(f == 0)
    def _():
        xo_ref[...] = x_ref[...]

    xn = xn_ref[...]
    gate = jnp.dot(xn, wg_ref[...].astype(BF16), preferred_element_type=F32)
    up = jnp.dot(xn, wu_ref[...].astype(BF16), preferred_element_type=F32)
    act = (_silu(gate) * up).astype(BF16)
    xo_ref[...] += jnp.dot(act, wd_ref[...].astype(BF16), preferred_element_type=F32)

    if xn_out_ref is not None:
        @pl.when(f == pl.num_programs(1) - 1)
        def _():
            xn_out_ref[...] = (_rms_rows(xo_ref[...]) * g_ref[...]).astype(xn_out_ref.dtype)


def _ffn(xn, wg, wu, wd, layer, x, g):
    s, d = x.shape
    hidden = wg.shape[2]
    row_block = lambda **kw: pl.BlockSpec((FFN_TM, d), lambda i, f: (i, 0), **kw)
    in_specs = [row_block(),
                pl.BlockSpec((None, d, FFN_TF), lambda i, f: (layer, 0, f)),
                pl.BlockSpec((None, d, FFN_TF), lambda i, f: (layer, 0, f)),
                pl.BlockSpec((None, FFN_TF, d), lambda i, f: (layer, f, 0)),
                row_block()]
    out_specs = [row_block()]
    out_shape = [jax.ShapeDtypeStruct((s, d), F32)]
    args = [xn, wg, wu, wd, x]
    if g is not None:
        in_specs.append(pl.BlockSpec((1, d), lambda i, f: (0, 0)))
        out_specs.append(row_block())
        out_shape.append(jax.ShapeDtypeStruct((s, d), BF16))
        args.append(g.reshape(1, d))
    outs = pl.pallas_call(
        _ffn_kernel,
        grid=(s // FFN_TM, hidden // FFN_TF),
        in_specs=in_specs,
        out_specs=out_specs,
        out_shape=out_shape,
        compiler_params=_compiler_params(("parallel", "arbitrary")),
        name="swiglu_ffn_residual_norm" if g is not None else "swiglu_ffn_residual",
    )(*args)
    return (outs[0], outs[1]) if g is not None else (outs[0], None)


def _rope_tables(seq, dim):
    inv = ROPE_THETA ** (-jnp.arange(0, dim, 2, dtype=F32) / dim)
    ang = jnp.arange(seq, dtype=F32)[:, None] * inv[None, :]
    cos = jnp.cos(ang)
    sin = jnp.sin(ang)
    reps = LANES // dim
    cos_l = jnp.tile(jnp.concatenate([cos, cos], axis=1), (1, reps))
    sin_l = jnp.tile(jnp.concatenate([-sin, sin], axis=1), (1, reps))
    return cos_l, sin_l


def kernel(x, norm_mix, norm_ffn, hyb_w_in, hyb_conv_w, hyb_w_out, diff_w_qkv, diff_q_norm,
           diff_k_norm, diff_lambda_q1, diff_lambda_k1, diff_lambda_q2, diff_lambda_k2,
           diff_subln, diff_w_out, ffn_w_gate, ffn_w_up, ffn_w_down):
    b, s, d = x.shape
    assert b == 1 and d == D_MODEL
    xs = x.reshape(s, d)
    cos_r, sin_r = _rope_tables(s, RET_DK)
    cos_a, sin_a = _rope_tables(s, DIFF_HEAD_DIM)

    xn = _rmsnorm(xs, norm_mix[0])
    for layer in range(DEPTH):
        j = layer // 2
        if layer % 2 == 0:
            qk_cols = 2 * RET_QK
            qk = _project_rope(xn, hyb_w_in, j, qk_cols, cos_r, sin_r, RET_DK // 2, None)
            rest = _project(xn, hyb_w_in, j, qk_cols, HYB_IN - qk_cols)
            mixed = _hybrid_mix(qk, rest, hyb_conv_w[j])
            w_out = hyb_w_out
        else:
            lambda_init = 0.8 - 0.6 * math.exp(-0.3 * layer)
            n_groups = DIFF_QK_COLS // DIFF_HEAD_DIM
            q_gain = diff_q_norm[j] * (LOG2_E * DIFF_HEAD_DIM ** -0.5)
            gain = jnp.concatenate([jnp.tile(q_gain, n_groups),
                                    jnp.tile(diff_k_norm[j], n_groups)]).reshape(1, -1)
            qk_cols = 2 * DIFF_QK_COLS
            qk = _project_rope(xn, diff_w_qkv, j, qk_cols, cos_a, sin_a,
                               DIFF_HEAD_DIM // 2, gain)
            v = _project(xn, diff_w_qkv, j, qk_cols, DIFF_QKV - qk_cols)
            score_bound = (ATT_BOUND_SLACK * DIFF_HEAD_DIM * jnp.max(jnp.abs(q_gain))
                           * jnp.max(jnp.abs(diff_k_norm[j]))).reshape(1).astype(F32)
            mixed = _diff_attention(qk, v, score_bound, diff_lambda_q1[j], diff_lambda_k1[j],
                                    diff_lambda_q2[j], diff_lambda_k2[j], diff_subln[j],
                                    lambda_init)
            w_out = diff_w_out
        xs, xn = _out_project(mixed, w_out, j, xs, norm_ffn[layer])
        g_next = norm_mix[layer + 1] if layer + 1 < DEPTH else None
        xs, xn = _ffn(xn, ffn_w_gate, ffn_w_up, ffn_w_down, layer, xs, g_next)
    return xs.reshape(b, s, d)
```

```python
import functools
import math

import jax
import jax.numpy as jnp
import numpy as np
from jax import lax
from jax.experimental import pallas as pl
from jax.experimental.pallas import tpu as pltpu

D_MODEL = 2048
DEPTH = 4
ROPE_THETA = 10000.0
NORM_EPS = 1e-6
RET_HEADS = 8
RET_DK = 64
RET_DV = 128
CONV_WIDTH = 1024
CONV_K = 3
DIFF_HEADS = 8
DIFF_HEAD_DIM = 128
FFN_HIDDEN = 5632
RET_QK = RET_HEADS * RET_DK
RET_V = RET_HEADS * RET_DV
HYB_IN = 2 * RET_QK + 2 * RET_V + 3 * CONV_WIDTH
DIFF_QKV = 6144
DIFF_QK_COLS = 2 * DIFF_HEADS * DIFF_HEAD_DIM

LANES = 128
SUBLANES = 8
VMEM_LIMIT_BYTES = 62 * 1024 * 1024

NORM_TM = 512
PROJ_TM = 2048
PROJ_TN = 512
OUT_TM = 512
FFN_TM = 1024
FFN_TF = 256
RET_TS = 256
ATT_TQ = 2048
ATT_TK = 512
ATT_UNROLL = 2
MASK_VALUE = -1e30
LOG2_E = math.log2(math.e)
ATT_BOUND_SLACK = 1.02
ATT_BOUND_MAX = 60.0

F32 = jnp.float32
BF16 = jnp.bfloat16


def _compiler_params(semantics):
    return pltpu.CompilerParams(dimension_semantics=semantics,
                                vmem_limit_bytes=VMEM_LIMIT_BYTES)


def _rms_rows(v):
    return v * lax.rsqrt(jnp.mean(v * v, axis=-1, keepdims=True) + NORM_EPS)


def _norm_kernel(x_ref, g_ref, o_ref):
    o_ref[...] = (_rms_rows(x_ref[...]) * g_ref[...]).astype(o_ref.dtype)


def _rmsnorm(x, g):
    s, d = x.shape
    return pl.pallas_call(
        _norm_kernel,
        grid=(s // NORM_TM,),
        in_specs=[pl.BlockSpec((NORM_TM, d), lambda i: (i, 0)),
                  pl.BlockSpec((1, d), lambda i: (0, 0))],
        out_specs=pl.BlockSpec((NORM_TM, d), lambda i: (i, 0)),
        out_shape=jax.ShapeDtypeStruct((s, d), BF16),
        compiler_params=_compiler_params(("parallel",)),
        name="rmsnorm",
    )(x, g.reshape(1, d))


def _rotate_half(blk, half):
    if 2 * half == LANES:
        return pltpu.roll(blk, half, axis=1)
    lane = lax.broadcasted_iota(jnp.int32, blk.shape, 1)
    first = (lane % (2 * half)) < half
    return jnp.where(first, pltpu.roll(blk, LANES - half, axis=1),
                     pltpu.roll(blk, half, axis=1))


def _proj_plain_kernel(x_ref, w_ref, o_ref):
    o_ref[...] = jnp.dot(x_ref[...], w_ref[...].astype(BF16),
                         preferred_element_type=F32).astype(o_ref.dtype)


def _proj_rope_kernel(x_ref, w_ref, cos_ref, sin_ref, *rest, rope_half):
    gain_ref, o_ref, acc_even_ref, acc_odd_ref = rest if len(rest) == 4 else (None,) + rest
    t = pl.program_id(0)
    n_groups = acc_even_ref.shape[1] // LANES
    k_chunk = x_ref.shape[1] // n_groups

    @pl.when(t == 0)
    def _():
        acc_odd_ref[...] = jnp.zeros_like(acc_odd_ref)

    def run(cur_ref, prev_ref):
        cur_ref[...] = jnp.dot(x_ref[...], w_ref[...].astype(BF16),
                               preferred_element_type=F32)
        for g in range(n_groups):
            cols = slice(g * LANES, (g + 1) * LANES)
            blk = prev_ref[:, cols]
            if gain_ref is not None:
                blk = _rms_rows(blk) * gain_ref[:, cols]
            out = blk * cos_ref[...] + _rotate_half(blk, rope_half) * sin_ref[...]
            o_ref[:, cols] = out.astype(o_ref.dtype)

    @pl.when(t % 2 == 0)
    def _():
        run(acc_even_ref, acc_odd_ref)

    @pl.when(t % 2 == 1)
    def _():
        run(acc_odd_ref, acc_even_ref)


def _project_rope(xn, w, layer, n_cols, cos, sin, rope_half, gain):
    s, d = xn.shape
    n_tiles = n_cols // PROJ_TN
    last = (s // PROJ_TM) * n_tiles - 1
    mm_row = lambda t: jnp.minimum(t, last) // n_tiles
    mm_col = lambda t: jnp.minimum(t, last) % n_tiles
    ep_row = lambda t: jnp.maximum(t - 1, 0) // n_tiles
    ep_col = lambda t: jnp.maximum(t - 1, 0) % n_tiles
    in_specs = [pl.BlockSpec((PROJ_TM, d), lambda t: (mm_row(t), 0)),
                pl.BlockSpec((None, d, PROJ_TN), lambda t: (layer, 0, mm_col(t))),
                pl.BlockSpec((PROJ_TM, LANES), lambda t: (ep_row(t), 0)),
                pl.BlockSpec((PROJ_TM, LANES), lambda t: (ep_row(t), 0))]
    args = [xn, w, cos, sin]
    if gain is not None:
        in_specs.append(pl.BlockSpec((1, PROJ_TN), lambda t: (0, ep_col(t))))
        args.append(gain)
    return pl.pallas_call(
        functools.partial(_proj_rope_kernel, rope_half=rope_half),
        grid=(last + 2,),
        in_specs=in_specs,
        out_specs=pl.BlockSpec((PROJ_TM, PROJ_TN), lambda t: (ep_row(t), ep_col(t))),
        out_shape=jax.ShapeDtypeStruct((s, n_cols), BF16),
        scratch_shapes=[pltpu.VMEM((PROJ_TM, PROJ_TN), F32),
                        pltpu.VMEM((PROJ_TM, PROJ_TN), F32)],
        compiler_params=_compiler_params(("arbitrary",)),
        name="proj_rope" if gain is None else "proj_qknorm_rope",
    )(*args)


def _project(xn, w, layer, col_start, n_cols):
    s, d = xn.shape
    first = col_start // PROJ_TN
    return pl.pallas_call(
        _proj_plain_kernel,
        grid=(s // PROJ_TM, n_cols // PROJ_TN),
        in_specs=[pl.BlockSpec((PROJ_TM, d), lambda m, j: (m, 0)),
                  pl.BlockSpec((None, d, PROJ_TN), lambda m, j: (layer, 0, first + j))],
        out_specs=pl.BlockSpec((PROJ_TM, PROJ_TN), lambda m, j: (m, j)),
        out_shape=jax.ShapeDtypeStruct((s, n_cols), BF16),
        compiler_params=_compiler_params(("parallel", "arbitrary")),
        name="proj_plain",
    )(xn, w)


def _retention_tables(ts):
    h = np.arange(RET_HEADS, dtype=np.float64)
    log_g = np.log(1.0 - np.exp2(-5.0 - h))
    idx = np.arange(ts, dtype=np.float64)
    dist = idx[:, None] - idx[None, :]
    scale = RET_DK ** -0.5
    decay = np.where(dist >= 0, np.exp(log_g[:, None, None] * np.maximum(dist, 0.0)), 0.0) * scale
    q_dec = np.exp(log_g[:, None] * (idx + 1.0))
    k_dec = np.exp(log_g[:, None] * (ts - 1.0 - idx)) * scale
    chunk_dec = np.exp(log_g * ts)
    q_dec = np.broadcast_to(q_dec[:, :, None], (RET_HEADS, ts, LANES))
    k_dec = np.broadcast_to(k_dec[:, :, None], (RET_HEADS, ts, LANES))
    return (jnp.asarray(decay, F32), jnp.asarray(q_dec, F32), jnp.asarray(k_dec, F32),
            [float(np.float32(c)) for c in chunk_dec])


def _silu(v):
    return v / (1.0 + jnp.exp(-v))


def _hybrid_kernel(q_ref, k_ref, v_ref, g_ref, cb_ref, cc_ref, cx_ref, convw_ref,
                   decay_ref, qdec_ref, kdec_ref, o_ref, state_ref, u_ref, *, chunk_dec):
    ts = q_ref.shape[0]

    @pl.when(pl.program_id(0) == 0)
    def _():
        state_ref[...] = jnp.zeros_like(state_ref)
        u_ref[0:SUBLANES, :] = jnp.zeros((SUBLANES, u_ref.shape[1]), F32)

    lane = lax.broadcasted_iota(jnp.int32, (ts, LANES), 1)
    for h in range(RET_HEADS):
        pair = slice((h // 2) * LANES, (h // 2 + 1) * LANES)
        head = slice(h * RET_DV, (h + 1) * RET_DV)
        in_head = (lane // RET_DK) == (h % 2)
        qm = jnp.where(in_head, q_ref[:, pair].astype(F32), 0.0)
        kp = k_ref[:, pair]
        vh = v_ref[:, head]
        s = lax.dot_general(qm.astype(BF16), kp, (((1,), (1,)), ((), ())),
                            preferred_element_type=F32)
        inner = (s * decay_ref[h]).astype(BF16)
        qd = (qm * qdec_ref[h]).astype(BF16)
        o = (jnp.dot(inner, vh, preferred_element_type=F32)
             + jnp.dot(qd, state_ref[h].astype(BF16), preferred_element_type=F32))
        kd = (kp.astype(F32) * kdec_ref[h]).astype(BF16)
        state_ref[h] = chunk_dec[h] * state_ref[h] + lax.dot_general(
            kd, vh, (((0,), (0,)), ((), ())), preferred_element_type=F32)
        gate = _silu(g_ref[:, head].astype(F32))
        o_ref[:, head] = (gate * _rms_rows(o)).astype(o_ref.dtype)

    u_ref[SUBLANES:SUBLANES + ts, :] = cc_ref[...].astype(F32) * cx_ref[...].astype(F32)
    y = (convw_ref[0:1, :] * u_ref[SUBLANES - 2:SUBLANES - 2 + ts, :]
         + convw_ref[1:2, :] * u_ref[SUBLANES - 1:SUBLANES - 1 + ts, :]
         + convw_ref[2:3, :] * u_ref[SUBLANES:SUBLANES + ts, :])
    o_ref[:, RET_V:] = (cb_ref[...].astype(F32) * y).astype(o_ref.dtype)
    u_ref[0:SUBLANES, :] = u_ref[ts:ts + SUBLANES, :]


def _hybrid_mix(qk, rest, conv_w):
    s = qk.shape[0]
    ts = RET_TS
    decay, q_dec, k_dec, chunk_dec = _retention_tables(ts)
    qk_blk = RET_QK
    w_blk = RET_V
    const3 = lambda i: (0, 0, 0)
    kernel = functools.partial(_hybrid_kernel, chunk_dec=chunk_dec)
    return pl.pallas_call(
        kernel,
        grid=(s // ts,),
        in_specs=[pl.BlockSpec((ts, qk_blk), lambda i: (i, 0)),
                  pl.BlockSpec((ts, qk_blk), lambda i: (i, 1)),
                  pl.BlockSpec((ts, w_blk), lambda i: (i, 0)),
                  pl.BlockSpec((ts, w_blk), lambda i: (i, 1)),
                  pl.BlockSpec((ts, w_blk), lambda i: (i, 2)),
                  pl.BlockSpec((ts, w_blk), lambda i: (i, 3)),
                  pl.BlockSpec((ts, w_blk), lambda i: (i, 4)),
                  pl.BlockSpec((SUBLANES, CONV_WIDTH), lambda i: (0, 0)),
                  pl.BlockSpec((RET_HEADS, ts, ts), const3),
                  pl.BlockSpec((RET_HEADS, ts, LANES), const3),
                  pl.BlockSpec((RET_HEADS, ts, LANES), const3)],
        out_specs=pl.BlockSpec((ts, RET_V + CONV_WIDTH), lambda i: (i, 0)),
        out_shape=jax.ShapeDtypeStruct((s, RET_V + CONV_WIDTH), BF16),
        scratch_shapes=[pltpu.VMEM((RET_HEADS, LANES, RET_DV), F32),
                        pltpu.VMEM((SUBLANES + ts, CONV_WIDTH), F32)],
        compiler_params=_compiler_params(("arbitrary",)),
        name="retention_conv",
    )(qk, qk, rest, rest, rest, rest, rest,
      jnp.pad(conv_w, ((0, SUBLANES - CONV_K), (0, 0))), decay, q_dec, k_dec)


def _lane_partial_sum(p):
    total = p[:, 0:LANES]
    for t in range(1, p.shape[1] // LANES):
        total = total + p[:, t * LANES:(t + 1) * LANES]
    return total


def _attn_kernel(bound_ref, q_ref, k_ref, v_ref, lq1_ref, lk1_ref, lq2_ref, lk2_ref,
                 subln_ref, o_ref, m_ref, l_ref, acc_ref, *, lambda_init):
    tq = q_ref.shape[0]
    tk = ATT_TK
    d = DIFF_HEAD_DIM
    qi = pl.program_id(1)
    bound = bound_ref[0]

    def scores(j, i, rows, masked):
        start = pl.multiple_of(j * tk, tk)
        s = lax.dot_general(q_ref[rows, i * d:(i + 1) * d],
                            k_ref[pl.ds(start, tk), i * d:(i + 1) * d],
                            (((1,), (1,)), ((), ())), preferred_element_type=F32)
        if masked:
            row = lax.broadcasted_iota(jnp.int32, s.shape, 0)
            col = lax.broadcasted_iota(jnp.int32, s.shape, 1)
            s = jnp.where(col <= row, s, MASK_VALUE)
        return s, v_ref[pl.ds(start, tk), :]

    def update_bounded(j, rows, masked, first=False):
        for i in range(2):
            s, v = scores(j, i, rows, masked)
            p = jnp.exp2(s - bound)
            pv = jnp.dot(p.astype(BF16), v, preferred_element_type=F32)
            if first:
                l_ref[i, rows, :] = _lane_partial_sum(p)
                acc_ref[i, rows, :] = pv
            else:
                l_ref[i, rows, :] += _lane_partial_sum(p)
                acc_ref[i, rows, :] += pv

    def update_online(j, rows, masked, first=False):
        for i in range(2):
            s, v = scores(j, i, rows, masked)
            m_cur = jnp.max(s, axis=1, keepdims=True)
            if first:
                m_new = jnp.broadcast_to(m_cur, (s.shape[0], LANES))
                p = jnp.exp2(s - m_cur)
                l_ref[i, rows, :] = _lane_partial_sum(p)
                acc_ref[i, rows, :] = jnp.dot(p.astype(BF16), v, preferred_element_type=F32)
            else:
                m_prev = m_ref[i, rows, :]
                m_new = jnp.maximum(m_prev, m_cur)
                alpha = jnp.exp2(m_prev - m_new)
                p = jnp.exp2(s - m_new[:, :1])
                l_ref[i, rows, :] = alpha * l_ref[i, rows, :] + _lane_partial_sum(p)
                acc_ref[i, rows, :] = alpha[:, :1] * acc_ref[i, rows, :] + jnp.dot(
                    p.astype(BF16), v, preferred_element_type=F32)
            m_ref[i, rows, :] = m_new

    def run(update):
        blocks_per_step = tq // tk
        all_rows = slice(0, tq)

        for r in range(blocks_per_step):
            update(blocks_per_step * qi + r, slice(r * tk, tq), masked=True, first=(r == 0))

        def body(t, carry):
            for u in range(ATT_UNROLL):
                update(ATT_UNROLL * t + u, all_rows, masked=False)
            return carry
        lax.fori_loop(0, (blocks_per_step // ATT_UNROLL) * qi, body, 0)

    @pl.when(bound <= ATT_BOUND_MAX)
    def _():
        run(update_bounded)

    @pl.when(bound > ATT_BOUND_MAX)
    def _():
        run(update_online)

    lam = (jnp.exp(jnp.sum(lq1_ref[...] * lk1_ref[...], axis=1, keepdims=True))
           - jnp.exp(jnp.sum(lq2_ref[...] * lk2_ref[...], axis=1, keepdims=True))
           + lambda_init)
    l0 = jnp.sum(l_ref[0], axis=1, keepdims=True)
    l1 = jnp.sum(l_ref[1], axis=1, keepdims=True)
    o = acc_ref[0] / l0 - lam * (acc_ref[1] / l1)
    o = _rms_rows(o) * subln_ref[...] * (1.0 - lambda_init)
    o_ref[...] = o.astype(o_ref.dtype)


def _diff_attention(qk, v, score_bound, lq1, lk1, lq2, lk2, subln, lambda_init):
    s = qk.shape[0]
    hd = 2 * DIFF_HEAD_DIM
    assert ATT_TQ % ATT_TK == 0
    vec = lambda a: a.reshape(1, -1).astype(F32)
    small = lambda w: pl.BlockSpec((1, w), lambda h, i: (0, 0))
    kernel = functools.partial(_attn_kernel, lambda_init=lambda_init)
    return pl.pallas_call(
        kernel,
        grid=(DIFF_HEADS, s // ATT_TQ),
        in_specs=[pl.BlockSpec(memory_space=pltpu.SMEM),
                  pl.BlockSpec((ATT_TQ, hd), lambda h, i: (i, h)),
                  pl.BlockSpec((s, hd), lambda h, i: (0, DIFF_HEADS + h)),
                  pl.BlockSpec((s, hd), lambda h, i: (0, h)),
                  small(DIFF_HEAD_DIM), small(DIFF_HEAD_DIM),
                  small(DIFF_HEAD_DIM), small(DIFF_HEAD_DIM), small(hd)],
        out_specs=pl.BlockSpec((ATT_TQ, hd), lambda h, i: (i, h)),
        out_shape=jax.ShapeDtypeStruct((s, DIFF_HEADS * hd), BF16),
        scratch_shapes=[pltpu.VMEM((2, ATT_TQ, LANES), F32),
                        pltpu.VMEM((2, ATT_TQ, LANES), F32),
                        pltpu.VMEM((2, ATT_TQ, hd), F32)],
        compiler_params=_compiler_params(("parallel", "arbitrary")),
        name="diff_attention",
    )(score_bound, qk, qk, v, vec(lq1), vec(lk1), vec(lq2), vec(lk2), vec(subln))


def _out_kernel(a_ref, w_ref, x_ref, g_ref, xo_ref, xn_ref, wb_ref):
    @pl.when(pl.program_id(0) == 0)
    def _():
        wb_ref[...] = w_ref[...].astype(BF16)

    x = x_ref[...] + jnp.dot(a_ref[...], wb_ref[...], preferred_element_type=F32)
    xo_ref[...] = x
    xn_ref[...] = (_rms_rows(x) * g_ref[...]).astype(xn_ref.dtype)


def _out_project(a, w, layer, x, g):
    s, d = x.shape
    k = a.shape[1]
    row = lambda i: (i, 0)
    return pl.pallas_call(
        _out_kernel,
        grid=(s // OUT_TM,),
        in_specs=[pl.BlockSpec((OUT_TM, k), row),
                  pl.BlockSpec((None, k, d), lambda i: (layer, 0, 0),
                               pipeline_mode=pl.Buffered(1)),
                  pl.BlockSpec((OUT_TM, d), row),
                  pl.BlockSpec((1, d), lambda i: (0, 0))],
        out_specs=[pl.BlockSpec((OUT_TM, d), row), pl.BlockSpec((OUT_TM, d), row)],
        out_shape=[jax.ShapeDtypeStruct((s, d), F32), jax.ShapeDtypeStruct((s, d), BF16)],
        scratch_shapes=[pltpu.VMEM((k, d), BF16)],
        compiler_params=_compiler_params(("arbitrary",)),
        name="out_proj_residual_norm",
    )(a, w, x, g.reshape(1, d))


def _ffn_kernel(xn_ref, wg_ref, wu_ref, wd_ref, x_ref, *rest):
    g_ref, xo_ref, xn_out_ref = rest if len(rest) == 3 else (None, rest[0], None)
    f = pl.program_id(1)

    @pl.when(f == 0)
    def _():
        xo_ref[...] = x_ref[...]

    xn = xn_ref[...]
    gate = jnp.dot(xn, wg_ref[...].astype(BF16), preferred_element_type=F32)
    up = jnp.dot(xn, wu_ref[...].astype(BF16), preferred_element_type=F32)
    act = (_silu(gate) * up).astype(BF16)
    xo_ref[...] += jnp.dot(act, wd_ref[...].astype(BF16), preferred_element_type=F32)

    if xn_out_ref is not None:
        @pl.when(f == pl.num_programs(1) - 1)
        def _():
            xn_out_ref[...] = (_rms_rows(xo_ref[...]) * g_ref[...]).astype(xn_out_ref.dtype)


def _ffn(xn, wg, wu, wd, layer, x, g):
    s, d = x.shape
    hidden = wg.shape[2]
    row_block = lambda **kw: pl.BlockSpec((FFN_TM, d), lambda i, f: (i, 0), **kw)
    in_specs = [row_block(),
                pl.BlockSpec((None, d, FFN_TF), lambda i, f: (layer, 0, f)),
                pl.BlockSpec((None, d, FFN_TF), lambda i, f: (layer, 0, f)),
                pl.BlockSpec((None, FFN_TF, d), lambda i, f: (layer, f, 0)),
                row_block()]
    out_specs = [row_block()]
    out_shape = [jax.ShapeDtypeStruct((s, d), F32)]
    args = [xn, wg, wu, wd, x]
    if g is not None:
        in_specs.append(pl.BlockSpec((1, d), lambda i, f: (0, 0)))
        out_specs.append(row_block())
        out_shape.append(jax.ShapeDtypeStruct((s, d), BF16))
        args.append(g.reshape(1, d))
    outs = pl.pallas_call(
        _ffn_kernel,
        grid=(s // FFN_TM, hidden // FFN_TF),
        in_specs=in_specs,
        out_specs=out_specs,
        out_shape=out_shape,
        compiler_params=_compiler_params(("parallel", "arbitrary")),
        name="swiglu_ffn_residual_norm" if g is not None else "swiglu_ffn_residual",
    )(*args)
    return (outs[0], outs[1]) if g is not None else (outs[0], None)


def _rope_tables(seq, dim):
    inv = ROPE_THETA ** (-jnp.arange(0, dim, 2, dtype=F32) / dim)
    ang = jnp.arange(seq, dtype=F32)[:, None] * inv[None, :]
    cos = jnp.cos(ang)
    sin = jnp.sin(ang)
    reps = LANES // dim
    cos_l = jnp.tile(jnp.concatenate([cos, cos], axis=1), (1, reps))
    sin_l = jnp.tile(jnp.concatenate([-sin, sin], axis=1), (1, reps))
    return cos_l, sin_l


def kernel(x, norm_mix, norm_ffn, hyb_w_in, hyb_conv_w, hyb_w_out, diff_w_qkv, diff_q_norm,
           diff_k_norm, diff_lambda_q1, diff_lambda_k1, diff_lambda_q2, diff_lambda_k2,
           diff_subln, diff_w_out, ffn_w_gate, ffn_w_up, ffn_w_down):
    b, s, d = x.shape
    assert b == 1 and d == D_MODEL
    xs = x.reshape(s, d)
    cos_r, sin_r = _rope_tables(s, RET_DK)
    cos_a, sin_a = _rope_tables(s, DIFF_HEAD_DIM)

    xn = _rmsnorm(xs, norm_mix[0])
    for layer in range(DEPTH):
        j = layer // 2
        if layer % 2 == 0:
            qk_cols = 2 * RET_QK
            qk = _project_rope(xn, hyb_w_in, j, qk_cols, cos_r, sin_r, RET_DK // 2, None)
            rest = _project(xn, hyb_w_in, j, qk_cols, HYB_IN - qk_cols)
            mixed = _hybrid_mix(qk, rest, hyb_conv_w[j])
            w_out = hyb_w_out
        else:
            lambda_init = 0.8 - 0.6 * math.exp(-0.3 * layer)
            n_groups = DIFF_QK_COLS // DIFF_HEAD_DIM
            q_gain = diff_q_norm[j] * (LOG2_E * DIFF_HEAD_DIM ** -0.5)
            gain = jnp.concatenate([jnp.tile(q_gain, n_groups),
                                    jnp.tile(diff_k_norm[j], n_groups)]).reshape(1, -1)
            qk_cols = 2 * DIFF_QK_COLS
            qk = _project_rope(xn, diff_w_qkv, j, qk_cols, cos_a, sin_a,
                               DIFF_HEAD_DIM // 2, gain)
            v = _project(xn, diff_w_qkv, j, qk_cols, DIFF_QKV - qk_cols)
            score_bound = (ATT_BOUND_SLACK * DIFF_HEAD_DIM * jnp.max(jnp.abs(q_gain))
                           * jnp.max(jnp.abs(diff_k_norm[j]))).reshape(1).astype(F32)
            mixed = _diff_attention(qk, v, score_bound, diff_lambda_q1[j], diff_lambda_k1[j],
                                    diff_lambda_q2[j], diff_lambda_k2[j], diff_subln[j],
                                    lambda_init)
            w_out = diff_w_out
        xs, xn = _out_project(mixed, w_out, j, xs, norm_ffn[layer])
        g_next = norm_mix[layer + 1] if layer + 1 < DEPTH else None
        xs, xn = _ffn(xn, ffn_w_gate, ffn_w_up, ffn_w_down, layer, xs, g_next)
    return xs.reshape(b, s, d)
```

```python
import functools
import math

import jax
import jax.numpy as jnp
import numpy as np
from jax import lax
from jax.experimental import pallas as pl
from jax.experimental.pallas import tpu as pltpu

D_MODEL = 2048
DEPTH = 4
ROPE_THETA = 10000.0
NORM_EPS = 1e-6
RET_HEADS = 8
RET_DK = 64
RET_DV = 128
CONV_WIDTH = 1024
CONV_K = 3
DIFF_HEADS = 8
DIFF_HEAD_DIM = 128
FFN_HIDDEN = 5632
RET_QK = RET_HEADS * RET_DK
RET_V = RET_HEADS * RET_DV
HYB_IN = 2 * RET_QK + 2 * RET_V + 3 * CONV_WIDTH
DIFF_QKV = 6144
DIFF_QK_COLS = 2 * DIFF_HEADS * DIFF_HEAD_DIM

LANES = 128
SUBLANES = 8
VMEM_LIMIT_BYTES = 62 * 1024 * 1024

NORM_TM = 512
PROJ_TM = 2048
PROJ_TN = 512
OUT_TM = 512
FFN_TM = 1024
FFN_TF = 256
RET_TS = 256
ATT_TQ = 2048
ATT_TK = 512
ATT_UNROLL = 4
MASK_VALUE = -1e30
LOG2_E = math.log2(math.e)
ATT_BOUND_SLACK = 1.02
ATT_BOUND_MAX = 60.0

F32 = jnp.float32
BF16 = jnp.bfloat16


def _compiler_params(semantics):
    return pltpu.CompilerParams(dimension_semantics=semantics,
                                vmem_limit_bytes=VMEM_LIMIT_BYTES)


def _rms_rows(v):
    return v * lax.rsqrt(jnp.mean(v * v, axis=-1, keepdims=True) + NORM_EPS)


def _norm_kernel(x_ref, g_ref, o_ref):
    o_ref[...] = (_rms_rows(x_ref[...]) * g_ref[...]).astype(o_ref.dtype)


def _rmsnorm(x, g):
    s, d = x.shape
    return pl.pallas_call(
        _norm_kernel,
        grid=(s // NORM_TM,),
        in_specs=[pl.BlockSpec((NORM_TM, d), lambda i: (i, 0)),
                  pl.BlockSpec((1, d), lambda i: (0, 0))],
        out_specs=pl.BlockSpec((NORM_TM, d), lambda i: (i, 0)),
        out_shape=jax.ShapeDtypeStruct((s, d), BF16),
        compiler_params=_compiler_params(("parallel",)),
        name="rmsnorm",
    )(x, g.reshape(1, d))


def _rotate_half(blk, half):
    if 2 * half == LANES:
        return pltpu.roll(blk, half, axis=1)
    lane = lax.broadcasted_iota(jnp.int32, blk.shape, 1)
    first = (lane % (2 * half)) < half
    return jnp.where(first, pltpu.roll(blk, LANES - half, axis=1),
                     pltpu.roll(blk, half, axis=1))


def _proj_plain_kernel(x_ref, w_ref, o_ref):
    o_ref[...] = jnp.dot(x_ref[...], w_ref[...].astype(BF16),
                         preferred_element_type=F32).astype(o_ref.dtype)


def _proj_rope_kernel(x_ref, w_ref, cos_ref, sin_ref, *rest, rope_half):
    gain_ref, o_ref, acc_even_ref, acc_odd_ref = rest if len(rest) == 4 else (None,) + rest
    t = pl.program_id(0)
    n_groups = acc_even_ref.shape[1] // LANES
    k_chunk = x_ref.shape[1] // n_groups

    @pl.when(t == 0)
    def _():
        acc_odd_ref[...] = jnp.zeros_like(acc_odd_ref)

    def run(cur_ref, prev_ref):
        cur_ref[...] = jnp.dot(x_ref[...], w_ref[...].astype(BF16),
                               preferred_element_type=F32)
        for g in range(n_groups):
            cols = slice(g * LANES, (g + 1) * LANES)
            blk = prev_ref[:, cols]
            if gain_ref is not None:
                blk = _rms_rows(blk) * gain_ref[:, cols]
            out = blk * cos_ref[...] + _rotate_half(blk, rope_half) * sin_ref[...]
            o_ref[:, cols] = out.astype(o_ref.dtype)

    @pl.when(t % 2 == 0)
    def _():
        run(acc_even_ref, acc_odd_ref)

    @pl.when(t % 2 == 1)
    def _():
        run(acc_odd_ref, acc_even_ref)


def _project_rope(xn, w, layer, n_cols, cos, sin, rope_half, gain):
    s, d = xn.shape
    n_tiles = n_cols // PROJ_TN
    last = (s // PROJ_TM) * n_tiles - 1
    mm_row = lambda t: jnp.minimum(t, last) // n_tiles
    mm_col = lambda t: jnp.minimum(t, last) % n_tiles
    ep_row = lambda t: jnp.maximum(t - 1, 0) // n_tiles
    ep_col = lambda t: jnp.maximum(t - 1, 0) % n_tiles
    in_specs = [pl.BlockSpec((PROJ_TM, d), lambda t: (mm_row(t), 0)),
                pl.BlockSpec((None, d, PROJ_TN), lambda t: (layer, 0, mm_col(t))),
                pl.BlockSpec((PROJ_TM, LANES), lambda t: (ep_row(t), 0)),
                pl.BlockSpec((PROJ_TM, LANES), lambda t: (ep_row(t), 0))]
    args = [xn, w, cos, sin]
    if gain is not None:
        in_specs.append(pl.BlockSpec((1, PROJ_TN), lambda t: (0, ep_col(t))))
        args.append(gain)
    return pl.pallas_call(
        functools.partial(_proj_rope_kernel, rope_half=rope_half),
        grid=(last + 2,),
        in_specs=in_specs,
        out_specs=pl.BlockSpec((PROJ_TM, PROJ_TN), lambda t: (ep_row(t), ep_col(t))),
        out_shape=jax.ShapeDtypeStruct((s, n_cols), BF16),
        scratch_shapes=[pltpu.VMEM((PROJ_TM, PROJ_TN), F32),
                        pltpu.VMEM((PROJ_TM, PROJ_TN), F32)],
        compiler_params=_compiler_params(("arbitrary",)),
        name="proj_rope" if gain is None else "proj_qknorm_rope",
    )(*args)


def _project(xn, w, layer, col_start, n_cols):
    s, d = xn.shape
    first = col_start // PROJ_TN
    return pl.pallas_call(
        _proj_plain_kernel,
        grid=(s // PROJ_TM, n_cols // PROJ_TN),
        in_specs=[pl.BlockSpec((PROJ_TM, d), lambda m, j: (m, 0)),
                  pl.BlockSpec((None, d, PROJ_TN), lambda m, j: (layer, 0, first + j))],
        out_specs=pl.BlockSpec((PROJ_TM, PROJ_TN), lambda m, j: (m, j)),
        out_shape=jax.ShapeDtypeStruct((s, n_cols), BF16),
        compiler_params=_compiler_params(("parallel", "arbitrary")),
        name="proj_plain",
    )(xn, w)


def _retention_tables(ts):
    h = np.arange(RET_HEADS, dtype=np.float64)
    log_g = np.log(1.0 - np.exp2(-5.0 - h))
    idx = np.arange(ts, dtype=np.float64)
    dist = idx[:, None] - idx[None, :]
    scale = RET_DK ** -0.5
    decay = np.where(dist >= 0, np.exp(log_g[:, None, None] * np.maximum(dist, 0.0)), 0.0) * scale
    q_dec = np.exp(log_g[:, None] * (idx + 1.0))
    k_dec = np.exp(log_g[:, None] * (ts - 1.0 - idx)) * scale
    chunk_dec = np.exp(log_g * ts)
    q_dec = np.broadcast_to(q_dec[:, :, None], (RET_HEADS, ts, LANES))
    k_dec = np.broadcast_to(k_dec[:, :, None], (RET_HEADS, ts, LANES))
    return (jnp.asarray(decay, F32), jnp.asarray(q_dec, F32), jnp.asarray(k_dec, F32),
            [float(np.float32(c)) for c in chunk_dec])


def _silu(v):
    return v / (1.0 + jnp.exp(-v))


def _hybrid_kernel(q_ref, k_ref, v_ref, g_ref, cb_ref, cc_ref, cx_ref, convw_ref,
                   decay_ref, qdec_ref, kdec_ref, o_ref, state_ref, u_ref, *, chunk_dec):
    ts = q_ref.shape[0]

    @pl.when(pl.program_id(0) == 0)
    def _():
        state_ref[...] = jnp.zeros_like(state_ref)
        u_ref[0:SUBLANES, :] = jnp.zeros((SUBLANES, u_ref.shape[1]), F32)

    lane = lax.broadcasted_iota(jnp.int32, (ts, LANES), 1)
    for h in range(RET_HEADS):
        pair = slice((h // 2) * LANES, (h // 2 + 1) * LANES)
        head = slice(h * RET_DV, (h + 1) * RET_DV)
        in_head = (lane // RET_DK) == (h % 2)
        qm = jnp.where(in_head, q_ref[:, pair].astype(F32), 0.0)
        kp = k_ref[:, pair]
        vh = v_ref[:, head]
        s = lax.dot_general(qm.astype(BF16), kp, (((1,), (1,)), ((), ())),
                            preferred_element_type=F32)
        inner = (s * decay_ref[h]).astype(BF16)
        qd = (qm * qdec_ref[h]).astype(BF16)
        o = (jnp.dot(inner, vh, preferred_element_type=F32)
             + jnp.dot(qd, state_ref[h].astype(BF16), preferred_element_type=F32))
        kd = (kp.astype(F32) * kdec_ref[h]).astype(BF16)
        state_ref[h] = chunk_dec[h] * state_ref[h] + lax.dot_general(
            kd, vh, (((0,), (0,)), ((), ())), preferred_element_type=F32)
        gate = _silu(g_ref[:, head].astype(F32))
        o_ref[:, head] = (gate * _rms_rows(o)).astype(o_ref.dtype)

    u_ref[SUBLANES:SUBLANES + ts, :] = cc_ref[...].astype(F32) * cx_ref[...].astype(F32)
    y = (convw_ref[0:1, :] * u_ref[SUBLANES - 2:SUBLANES - 2 + ts, :]
         + convw_ref[1:2, :] * u_ref[SUBLANES - 1:SUBLANES - 1 + ts, :]
         + convw_ref[2:3, :] * u_ref[SUBLANES:SUBLANES + ts, :])
    o_ref[:, RET_V:] = (cb_ref[...].astype(F32) * y).astype(o_ref.dtype)
    u_ref[0:SUBLANES, :] = u_ref[ts:ts + SUBLANES, :]


def _hybrid_mix(qk, rest, conv_w):
    s = qk.shape[0]
    ts = RET_TS
    decay, q_dec, k_dec, chunk_dec = _retention_tables(ts)
    qk_blk = RET_QK
    w_blk = RET_V
    const3 = lambda i: (0, 0, 0)
    kernel = functools.partial(_hybrid_kernel, chunk_dec=chunk_dec)
    return pl.pallas_call(
        kernel,
        grid=(s // ts,),
        in_specs=[pl.BlockSpec((ts, qk_blk), lambda i: (i, 0)),
                  pl.BlockSpec((ts, qk_blk), lambda i: (i, 1)),
                  pl.BlockSpec((ts, w_blk), lambda i: (i, 0)),
                  pl.BlockSpec((ts, w_blk), lambda i: (i, 1)),
                  pl.BlockSpec((ts, w_blk), lambda i: (i, 2)),
                  pl.BlockSpec((ts, w_blk), lambda i: (i, 3)),
                  pl.BlockSpec((ts, w_blk), lambda i: (i, 4)),
                  pl.BlockSpec((SUBLANES, CONV_WIDTH), lambda i: (0, 0)),
                  pl.BlockSpec((RET_HEADS, ts, ts), const3),
                  pl.BlockSpec((RET_HEADS, ts, LANES), const3),
                  pl.BlockSpec((RET_HEADS, ts, LANES), const3)],
        out_specs=pl.BlockSpec((ts, RET_V + CONV_WIDTH), lambda i: (i, 0)),
        out_shape=jax.ShapeDtypeStruct((s, RET_V + CONV_WIDTH), BF16),
        scratch_shapes=[pltpu.VMEM((RET_HEADS, LANES, RET_DV), F32),
                        pltpu.VMEM((SUBLANES + ts, CONV_WIDTH), F32)],
        compiler_params=_compiler_params(("arbitrary",)),
        name="retention_conv",
    )(qk, qk, rest, rest, rest, rest, rest,
      jnp.pad(conv_w, ((0, SUBLANES - CONV_K), (0, 0))), decay, q_dec, k_dec)


def _lane_partial_sum(p):
    total = p[:, 0:LANES]
    for t in range(1, p.shape[1] // LANES):
        total = total + p[:, t * LANES:(t + 1) * LANES]
    return total


def _attn_kernel(bound_ref, q_ref, k_ref, v_ref, lq1_ref, lk1_ref, lq2_ref, lk2_ref,
                 subln_ref, o_ref, m_ref, l_ref, acc_ref, *, lambda_init):
    tq = q_ref.shape[0]
    tk = ATT_TK
    d = DIFF_HEAD_DIM
    qi = pl.program_id(1)
    bound = bound_ref[0]

    def scores(j, i, rows, masked):
        start = pl.multiple_of(j * tk, tk)
        s = lax.dot_general(q_ref[rows, i * d:(i + 1) * d],
                            k_ref[pl.ds(start, tk), i * d:(i + 1) * d],
                            (((1,), (1,)), ((), ())), preferred_element_type=F32)
        if masked:
            row = lax.broadcasted_iota(jnp.int32, s.shape, 0)
            col = lax.broadcasted_iota(jnp.int32, s.shape, 1)
            s = jnp.where(col <= row, s, MASK_VALUE)
        return s, v_ref[pl.ds(start, tk), :]

    def update_bounded(j, rows, masked, first=False):
        for i in range(2):
            s, v = scores(j, i, rows, masked)
            p = jnp.exp2(s - bound)
            pv = jnp.dot(p.astype(BF16), v, preferred_element_type=F32)
            if first:
                l_ref[i, rows, :] = _lane_partial_sum(p)
                acc_ref[i, rows, :] = pv
            else:
                l_ref[i, rows, :] += _lane_partial_sum(p)
                acc_ref[i, rows, :] += pv

    def update_online(j, rows, masked, first=False):
        for i in range(2):
            s, v = scores(j, i, rows, masked)
            m_cur = jnp.max(s, axis=1, keepdims=True)
            if first:
                m_new = jnp.broadcast_to(m_cur, (s.shape[0], LANES))
                p = jnp.exp2(s - m_cur)
                l_ref[i, rows, :] = _lane_partial_sum(p)
                acc_ref[i, rows, :] = jnp.dot(p.astype(BF16), v, preferred_element_type=F32)
            else:
                m_prev = m_ref[i, rows, :]
                m_new = jnp.maximum(m_prev, m_cur)
                alpha = jnp.exp2(m_prev - m_new)
                p = jnp.exp2(s - m_new[:, :1])
                l_ref[i, rows, :] = alpha * l_ref[i, rows, :] + _lane_partial_sum(p)
                acc_ref[i, rows, :] = alpha[:, :1] * acc_ref[i, rows, :] + jnp.dot(
                    p.astype(BF16), v, preferred_element_type=F32)
            m_ref[i, rows, :] = m_new

    def run(update, unroll):
        blocks_per_step = tq // tk
        all_rows = slice(0, tq)

        for r in range(blocks_per_step):
            update(blocks_per_step * qi + r, slice(r * tk, tq), masked=True, first=(r == 0))

        def body(t, carry):
            for u in range(unroll):
                update(unroll * t + u, all_rows, masked=False)
            return carry
        lax.fori_loop(0, (blocks_per_step // unroll) * qi, body, 0)

    @pl.when(bound <= ATT_BOUND_MAX)
    def _():
        run(update_bounded, ATT_UNROLL)

    @pl.when(bound > ATT_BOUND_MAX)
    def _():
        run(update_online, 1)

    lam = (jnp.exp(jnp.sum(lq1_ref[...] * lk1_ref[...], axis=1, keepdims=True))
           - jnp.exp(jnp.sum(lq2_ref[...] * lk2_ref[...], axis=1, keepdims=True))
           + lambda_init)
    l0 = jnp.sum(l_ref[0], axis=1, keepdims=True)
    l1 = jnp.sum(l_ref[1], axis=1, keepdims=True)
    o = acc_ref[0] / l0 - lam * (acc_ref[1] / l1)
    o = _rms_rows(o) * subln_ref[...] * (1.0 - lambda_init)
    o_ref[...] = o.astype(o_ref.dtype)


def _diff_attention(qk, v, score_bound, lq1, lk1, lq2, lk2, subln, lambda_init):
    s = qk.shape[0]
    hd = 2 * DIFF_HEAD_DIM
    assert ATT_TQ % ATT_TK == 0
    vec = lambda a: a.reshape(1, -1).astype(F32)
    small = lambda w: pl.BlockSpec((1, w), lambda h, i: (0, 0))
    kernel = functools.partial(_attn_kernel, lambda_init=lambda_init)
    return pl.pallas_call(
        kernel,
        grid=(DIFF_HEADS, s // ATT_TQ),
        in_specs=[pl.BlockSpec(memory_space=pltpu.SMEM),
                  pl.BlockSpec((ATT_TQ, hd), lambda h, i: (i, h)),
                  pl.BlockSpec((s, hd), lambda h, i: (0, DIFF_HEADS + h)),
                  pl.BlockSpec((s, hd), lambda h, i: (0, h)),
                  small(DIFF_HEAD_DIM), small(DIFF_HEAD_DIM),
                  small(DIFF_HEAD_DIM), small(DIFF_HEAD_DIM), small(hd)],
        out_specs=pl.BlockSpec((ATT_TQ, hd), lambda h, i: (i, h)),
        out_shape=jax.ShapeDtypeStruct((s, DIFF_HEADS * hd), BF16),
        scratch_shapes=[pltpu.VMEM((2, ATT_TQ, LANES), F32),
                        pltpu.VMEM((2, ATT_TQ, LANES), F32),
                        pltpu.VMEM((2, ATT_TQ, hd), F32)],
        compiler_params=_compiler_params(("parallel", "arbitrary")),
        name="diff_attention",
    )(score_bound, qk, qk, v, vec(lq1), vec(lk1), vec(lq2), vec(lk2), vec(subln))


def _out_kernel(a_ref, w_ref, x_ref, g_ref, xo_ref, xn_ref, wb_ref):
    @pl.when(pl.program_id(0) == 0)
    def _():
        wb_ref[...] = w_ref[...].astype(BF16)

    x = x_ref[...] + jnp.dot(a_ref[...], wb_ref[...], preferred_element_type=F32)
    xo_ref[...] = x
    xn_ref[...] = (_rms_rows(x) * g_ref[...]).astype(xn_ref.dtype)


def _out_project(a, w, layer, x, g):
    s, d = x.shape
    k = a.shape[1]
    row = lambda i: (i, 0)
    return pl.pallas_call(
        _out_kernel,
        grid=(s // OUT_TM,),
        in_specs=[pl.BlockSpec((OUT_TM, k), row),
                  pl.BlockSpec((None, k, d), lambda i: (layer, 0, 0),
                               pipeline_mode=pl.Buffered(1)),
                  pl.BlockSpec((OUT_TM, d), row),
                  pl.BlockSpec((1, d), lambda i: (0, 0))],
        out_specs=[pl.BlockSpec((OUT_TM, d), row), pl.BlockSpec((OUT_TM, d), row)],
        out_shape=[jax.ShapeDtypeStruct((s, d), F32), jax.ShapeDtypeStruct((s, d), BF16)],
        scratch_shapes=[pltpu.VMEM((k, d), BF16)],
        compiler_params=_compiler_params(("arbitrary",)),
        name="out_proj_residual_norm",
    )(a, w, x, g.reshape(1, d))


def _ffn_kernel(xn_ref, wg_ref, wu_ref, wd_ref, x_ref, *rest):
    g_ref, xo_ref, xn_out_ref = rest if len(rest) == 3 else (None, rest[0], None)
    f = pl.program_id(1)

    @pl.when(f == 0)
    def _():
        xo_ref[...] = x_ref[...]

    xn = xn_ref[...]
    gate = jnp.dot(xn, wg_ref[...].astype(BF16), preferred_element_type=F32)
    up = jnp.dot(xn, wu_ref[...].astype(BF16), preferred_element_type=F32)
    act = (_silu(gate) * up).astype(BF16)
    xo_ref[...] += jnp.dot(act, wd_ref[...].astype(BF16), preferred_element_type=F32)

    if xn_out_ref is not None:
        @pl.when(f == pl.num_programs(1) - 1)
        def _():
            xn_out_ref[...] = (_rms_rows(xo_ref[...]) * g_ref[...]).astype(xn_out_ref.dtype)


def _ffn(xn, wg, wu, wd, layer, x, g):
    s, d = x.shape
    hidden = wg.shape[2]
    row_block = lambda **kw: pl.BlockSpec((FFN_TM, d), lambda i, f: (i, 0), **kw)
    in_specs = [row_block(),
                pl.BlockSpec((None, d, FFN_TF), lambda i, f: (layer, 0, f)),
                pl.BlockSpec((None, d, FFN_TF), lambda i, f: (layer, 0, f)),
                pl.BlockSpec((None, FFN_TF, d), lambda i, f: (layer, f, 0)),
                row_block()]
    out_specs = [row_block()]
    out_shape = [jax.ShapeDtypeStruct((s, d), F32)]
    args = [xn, wg, wu, wd, x]
    if g is not None:
        in_specs.append(pl.BlockSpec((1, d), lambda i, f: (0, 0)))
        out_specs.append(row_block())
        out_shape.append(jax.ShapeDtypeStruct((s, d), BF16))
        args.append(g.reshape(1, d))
    outs = pl.pallas_call(
        _ffn_kernel,
        grid=(s // FFN_TM, hidden // FFN_TF),
        in_specs=in_specs,
        out_specs=out_specs,
        out_shape=out_shape,
        compiler_params=_compiler_params(("parallel", "arbitrary")),
        name="swiglu_ffn_residual_norm" if g is not None else "swiglu_ffn_residual",
    )(*args)
    return (outs[0], outs[1]) if g is not None else (outs[0], None)


def _rope_tables(seq, dim):
    inv = ROPE_THETA ** (-jnp.arange(0, dim, 2, dtype=F32) / dim)
    ang = jnp.arange(seq, dtype=F32)[:, None] * inv[None, :]
    cos = jnp.cos(ang)
    sin = jnp.sin(ang)
    reps = LANES // dim
    cos_l = jnp.tile(jnp.concatenate([cos, cos], axis=1), (1, reps))
    sin_l = jnp.tile(jnp.concatenate([-sin, sin], axis=1), (1, reps))
    return cos_l, sin_l


def kernel(x, norm_mix, norm_ffn, hyb_w_in, hyb_conv_w, hyb_w_out, diff_w_qkv, diff_q_norm,
           diff_k_norm, diff_lambda_q1, diff_lambda_k1, diff_lambda_q2, diff_lambda_k2,
           diff_subln, diff_w_out, ffn_w_gate, ffn_w_up, ffn_w_down):
    b, s, d = x.shape
    assert b == 1 and d == D_MODEL
    xs = x.reshape(s, d)
    cos_r, sin_r = _rope_tables(s, RET_DK)
    cos_a, sin_a = _rope_tables(s, DIFF_HEAD_DIM)

    xn = _rmsnorm(xs, norm_mix[0])
    for layer in range(DEPTH):
        j = layer // 2
        if layer % 2 == 0:
            qk_cols = 2 * RET_QK
            qk = _project_rope(xn, hyb_w_in, j, qk_cols, cos_r, sin_r, RET_DK // 2, None)
            rest = _project(xn, hyb_w_in, j, qk_cols, HYB_IN - qk_cols)
            mixed = _hybrid_mix(qk, rest, hyb_conv_w[j])
            w_out = hyb_w_out
        else:
            lambda_init = 0.8 - 0.6 * math.exp(-0.3 * layer)
            n_groups = DIFF_QK_COLS // DIFF_HEAD_DIM
            q_gain = diff_q_norm[j] * (LOG2_E * DIFF_HEAD_DIM ** -0.5)
            gain = jnp.concatenate([jnp.tile(q_gain, n_groups),
                                    jnp.tile(diff_k_norm[j], n_groups)]).reshape(1, -1)
            qk_cols = 2 * DIFF_QK_COLS
            qk = _project_rope(xn, diff_w_qkv, j, qk_cols, cos_a, sin_a,
                               DIFF_HEAD_DIM // 2, gain)
            v = _project(xn, diff_w_qkv, j, qk_cols, DIFF_QKV - qk_cols)
            score_bound = (ATT_BOUND_SLACK * DIFF_HEAD_DIM * jnp.max(jnp.abs(q_gain))
                           * jnp.max(jnp.abs(diff_k_norm[j]))).reshape(1).astype(F32)
            mixed = _diff_attention(qk, v, score_bound, diff_lambda_q1[j], diff_lambda_k1[j],
                                    diff_lambda_q2[j], diff_lambda_k2[j], diff_subln[j],
                                    lambda_init)
            w_out = diff_w_out
        xs, xn = _out_project(mixed, w_out, j, xs, norm_ffn[layer])
        g_next = norm_mix[layer + 1] if layer + 1 < DEPTH else None
        xs, xn = _ffn(xn, ffn_w_gate, ffn_w_up, ffn_w_down, layer, xs, g_next)
    return xs.reshape(b, s, d)
```

```python
import functools
import math

import jax
import jax.numpy as jnp
import numpy as np
from jax import lax
from jax.experimental import pallas as pl
from jax.experimental.pallas import tpu as pltpu

D_MODEL = 2048
DEPTH = 4
ROPE_THETA = 10000.0
NORM_EPS = 1e-6
RET_HEADS = 8
RET_DK = 64
RET_DV = 128
CONV_WIDTH = 1024
CONV_K = 3
DIFF_HEADS = 8
DIFF_HEAD_DIM = 128
FFN_HIDDEN = 5632
RET_QK = RET_HEADS * RET_DK
RET_V = RET_HEADS * RET_DV
HYB_IN = 2 * RET_QK + 2 * RET_V + 3 * CONV_WIDTH
DIFF_QKV = 6144
DIFF_QK_COLS = 2 * DIFF_HEADS * DIFF_HEAD_DIM

LANES = 128
SUBLANES = 8
VMEM_LIMIT_BYTES = 62 * 1024 * 1024

NORM_TM = 512
PROJ_TM = 2048
PROJ_TN = 512
PROJ_PLAIN_TN = 1024
OUT_TM = 512
FFN_TM = 1024
FFN_TF = 256
RET_TS = 256
ATT_TQ = 2048
ATT_TK = 512
ATT_UNROLL = 4
MASK_VALUE = -1e30
LOG2_E = math.log2(math.e)
ATT_BOUND_SLACK = 1.02
ATT_BOUND_MAX = 60.0

F32 = jnp.float32
BF16 = jnp.bfloat16


def _compiler_params(semantics):
    return pltpu.CompilerParams(dimension_semantics=semantics,
                                vmem_limit_bytes=VMEM_LIMIT_BYTES)


def _rms_rows(v):
    return v * lax.rsqrt(jnp.mean(v * v, axis=-1, keepdims=True) + NORM_EPS)


def _norm_kernel(x_ref, g_ref, o_ref):
    o_ref[...] = (_rms_rows(x_ref[...]) * g_ref[...]).astype(o_ref.dtype)


def _rmsnorm(x, g):
    s, d = x.shape
    return pl.pallas_call(
        _norm_kernel,
        grid=(s // NORM_TM,),
        in_specs=[pl.BlockSpec((NORM_TM, d), lambda i: (i, 0)),
                  pl.BlockSpec((1, d), lambda i: (0, 0))],
        out_specs=pl.BlockSpec((NORM_TM, d), lambda i: (i, 0)),
        out_shape=jax.ShapeDtypeStruct((s, d), BF16),
        compiler_params=_compiler_params(("parallel",)),
        name="rmsnorm",
    )(x, g.reshape(1, d))


def _rotate_half(blk, half):
    if 2 * half == LANES:
        return pltpu.roll(blk, half, axis=1)
    lane = lax.broadcasted_iota(jnp.int32, blk.shape, 1)
    first = (lane % (2 * half)) < half
    return jnp.where(first, pltpu.roll(blk, LANES - half, axis=1),
                     pltpu.roll(blk, half, axis=1))


def _proj_plain_kernel(x_ref, w_ref, o_ref):
    o_ref[...] = jnp.dot(x_ref[...], w_ref[...].astype(BF16),
                         preferred_element_type=F32).astype(o_ref.dtype)


def _proj_rope_kernel(x_ref, w_ref, cos_ref, sin_ref, *rest, rope_half):
    gain_ref, o_ref, acc_even_ref, acc_odd_ref = rest if len(rest) == 4 else (None,) + rest
    t = pl.program_id(0)
    n_groups = acc_even_ref.shape[1] // LANES
    k_chunk = x_ref.shape[1] // n_groups

    @pl.when(t == 0)
    def _():
        acc_odd_ref[...] = jnp.zeros_like(acc_odd_ref)

    def run(cur_ref, prev_ref):
        cur_ref[...] = jnp.dot(x_ref[...], w_ref[...].astype(BF16),
                               preferred_element_type=F32)
        for g in range(n_groups):
            cols = slice(g * LANES, (g + 1) * LANES)
            blk = prev_ref[:, cols]
            if gain_ref is not None:
                blk = _rms_rows(blk) * gain_ref[:, cols]
            out = blk * cos_ref[...] + _rotate_half(blk, rope_half) * sin_ref[...]
            o_ref[:, cols] = out.astype(o_ref.dtype)

    @pl.when(t % 2 == 0)
    def _():
        run(acc_even_ref, acc_odd_ref)

    @pl.when(t % 2 == 1)
    def _():
        run(acc_odd_ref, acc_even_ref)


def _project_rope(xn, w, layer, n_cols, cos, sin, rope_half, gain):
    s, d = xn.shape
    n_tiles = n_cols // PROJ_TN
    last = (s // PROJ_TM) * n_tiles - 1
    mm_row = lambda t: jnp.minimum(t, last) // n_tiles
    mm_col = lambda t: jnp.minimum(t, last) % n_tiles
    ep_row = lambda t: jnp.maximum(t - 1, 0) // n_tiles
    ep_col = lambda t: jnp.maximum(t - 1, 0) % n_tiles
    in_specs = [pl.BlockSpec((PROJ_TM, d), lambda t: (mm_row(t), 0)),
                pl.BlockSpec((None, d, PROJ_TN), lambda t: (layer, 0, mm_col(t))),
                pl.BlockSpec((PROJ_TM, LANES), lambda t: (ep_row(t), 0)),
                pl.BlockSpec((PROJ_TM, LANES), lambda t: (ep_row(t), 0))]
    args = [xn, w, cos, sin]
    if gain is not None:
        in_specs.append(pl.BlockSpec((1, PROJ_TN), lambda t: (0, ep_col(t))))
        args.append(gain)
    return pl.pallas_call(
        functools.partial(_proj_rope_kernel, rope_half=rope_half),
        grid=(last + 2,),
        in_specs=in_specs,
        out_specs=pl.BlockSpec((PROJ_TM, PROJ_TN), lambda t: (ep_row(t), ep_col(t))),
        out_shape=jax.ShapeDtypeStruct((s, n_cols), BF16),
        scratch_shapes=[pltpu.VMEM((PROJ_TM, PROJ_TN), F32),
                        pltpu.VMEM((PROJ_TM, PROJ_TN), F32)],
        compiler_params=_compiler_params(("arbitrary",)),
        name="proj_rope" if gain is None else "proj_qknorm_rope",
    )(*args)


def _project(xn, w, layer, col_start, n_cols):
    s, d = xn.shape
    tn = PROJ_PLAIN_TN
    first = col_start // tn
    return pl.pallas_call(
        _proj_plain_kernel,
        grid=(s // PROJ_TM, n_cols // tn),
        in_specs=[pl.BlockSpec((PROJ_TM, d), lambda m, j: (m, 0)),
                  pl.BlockSpec((None, d, tn), lambda m, j: (layer, 0, first + j))],
        out_specs=pl.BlockSpec((PROJ_TM, tn), lambda m, j: (m, j)),
        out_shape=jax.ShapeDtypeStruct((s, n_cols), BF16),
        compiler_params=_compiler_params(("parallel", "arbitrary")),
        name="proj_plain",
    )(xn, w)


def _retention_tables(ts):
    h = np.arange(RET_HEADS, dtype=np.float64)
    log_g = np.log(1.0 - np.exp2(-5.0 - h))
    idx = np.arange(ts, dtype=np.float64)
    dist = idx[:, None] - idx[None, :]
    scale = RET_DK ** -0.5
    decay = np.where(dist >= 0, np.exp(log_g[:, None, None] * np.maximum(dist, 0.0)), 0.0) * scale
    q_dec = np.exp(log_g[:, None] * (idx + 1.0))
    k_dec = np.exp(log_g[:, None] * (ts - 1.0 - idx)) * scale
    chunk_dec = np.exp(log_g * ts)
    q_dec = np.broadcast_to(q_dec[:, :, None], (RET_HEADS, ts, LANES))
    k_dec = np.broadcast_to(k_dec[:, :, None], (RET_HEADS, ts, LANES))
    return (jnp.asarray(decay, F32), jnp.asarray(q_dec, F32), jnp.asarray(k_dec, F32),
            [float(np.float32(c)) for c in chunk_dec])


def _silu(v):
    return v / (1.0 + jnp.exp(-v))


def _hybrid_kernel(q_ref, k_ref, v_ref, g_ref, cb_ref, cc_ref, cx_ref, convw_ref,
                   decay_ref, qdec_ref, kdec_ref, o_ref, state_ref, u_ref, *, chunk_dec):
    ts = q_ref.shape[0]

    @pl.when(pl.program_id(0) == 0)
    def _():
        state_ref[...] = jnp.zeros_like(state_ref)
        u_ref[0:SUBLANES, :] = jnp.zeros((SUBLANES, u_ref.shape[1]), F32)

    lane = lax.broadcasted_iota(jnp.int32, (ts, LANES), 1)
    for h in range(RET_HEADS):
        pair = slice((h // 2) * LANES, (h // 2 + 1) * LANES)
        head = slice(h * RET_DV, (h + 1) * RET_DV)
        in_head = (lane // RET_DK) == (h % 2)
        qm = jnp.where(in_head, q_ref[:, pair].astype(F32), 0.0)
        kp = k_ref[:, pair]
        vh = v_ref[:, head]
        s = lax.dot_general(qm.astype(BF16), kp, (((1,), (1,)), ((), ())),
                            preferred_element_type=F32)
        inner = (s * decay_ref[h]).astype(BF16)
        qd = (qm * qdec_ref[h]).astype(BF16)
        o = (jnp.dot(inner, vh, preferred_element_type=F32)
             + jnp.dot(qd, state_ref[h].astype(BF16), preferred_element_type=F32))
        kd = (kp.astype(F32) * kdec_ref[h]).astype(BF16)
        state_ref[h] = chunk_dec[h] * state_ref[h] + lax.dot_general(
            kd, vh, (((0,), (0,)), ((), ())), preferred_element_type=F32)
        gate = _silu(g_ref[:, head].astype(F32))
        o_ref[:, head] = (gate * _rms_rows(o)).astype(o_ref.dtype)

    u_ref[SUBLANES:SUBLANES + ts, :] = cc_ref[...].astype(F32) * cx_ref[...].astype(F32)
    y = (convw_ref[0:1, :] * u_ref[SUBLANES - 2:SUBLANES - 2 + ts, :]
         + convw_ref[1:2, :] * u_ref[SUBLANES - 1:SUBLANES - 1 + ts, :]
         + convw_ref[2:3, :] * u_ref[SUBLANES:SUBLANES + ts, :])
    o_ref[:, RET_V:] = (cb_ref[...].astype(F32) * y).astype(o_ref.dtype)
    u_ref[0:SUBLANES, :] = u_ref[ts:ts + SUBLANES, :]


def _hybrid_mix(qk, rest, conv_w):
    s = qk.shape[0]
    ts = RET_TS
    decay, q_dec, k_dec, chunk_dec = _retention_tables(ts)
    qk_blk = RET_QK
    w_blk = RET_V
    const3 = lambda i: (0, 0, 0)
    kernel = functools.partial(_hybrid_kernel, chunk_dec=chunk_dec)
    return pl.pallas_call(
        kernel,
        grid=(s // ts,),
        in_specs=[pl.BlockSpec((ts, qk_blk), lambda i: (i, 0)),
                  pl.BlockSpec((ts, qk_blk), lambda i: (i, 1)),
                  pl.BlockSpec((ts, w_blk), lambda i: (i, 0)),
                  pl.BlockSpec((ts, w_blk), lambda i: (i, 1)),
                  pl.BlockSpec((ts, w_blk), lambda i: (i, 2)),
                  pl.BlockSpec((ts, w_blk), lambda i: (i, 3)),
                  pl.BlockSpec((ts, w_blk), lambda i: (i, 4)),
                  pl.BlockSpec((SUBLANES, CONV_WIDTH), lambda i: (0, 0)),
                  pl.BlockSpec((RET_HEADS, ts, ts), const3),
                  pl.BlockSpec((RET_HEADS, ts, LANES), const3),
                  pl.BlockSpec((RET_HEADS, ts, LANES), const3)],
        out_specs=pl.BlockSpec((ts, RET_V + CONV_WIDTH), lambda i: (i, 0)),
        out_shape=jax.ShapeDtypeStruct((s, RET_V + CONV_WIDTH), BF16),
        scratch_shapes=[pltpu.VMEM((RET_HEADS, LANES, RET_DV), F32),
                        pltpu.VMEM((SUBLANES + ts, CONV_WIDTH), F32)],
        compiler_params=_compiler_params(("arbitrary",)),
        name="retention_conv",
    )(qk, qk, rest, rest, rest, rest, rest,
      jnp.pad(conv_w, ((0, SUBLANES - CONV_K), (0, 0))), decay, q_dec, k_dec)


def _lane_partial_sum(p):
    total = p[:, 0:LANES]
    for t in range(1, p.shape[1] // LANES):
        total = total + p[:, t * LANES:(t + 1) * LANES]
    return total


def _attn_kernel(bound_ref, q_ref, k_ref, v_ref, lq1_ref, lk1_ref, lq2_ref, lk2_ref,
                 subln_ref, o_ref, m_ref, l_ref, acc_ref, *, lambda_init):
    tq = q_ref.shape[0]
    tk = ATT_TK
    d = DIFF_HEAD_DIM
    qi = pl.program_id(1)
    bound = bound_ref[0]

    def scores(j, i, rows, masked):
        start = pl.multiple_of(j * tk, tk)
        s = lax.dot_general(q_ref[rows, i * d:(i + 1) * d],
                            k_ref[pl.ds(start, tk), i * d:(i + 1) * d],
                            (((1,), (1,)), ((), ())), preferred_element_type=F32)
        if masked:
            row = lax.broadcasted_iota(jnp.int32, s.shape, 0)
            col = lax.broadcasted_iota(jnp.int32, s.shape, 1)
            s = jnp.where(col <= row, s, MASK_VALUE)
        return s, v_ref[pl.ds(start, tk), :]

    def update_bounded(j, rows, masked, first=False):
        for i in range(2):
            s, v = scores(j, i, rows, masked)
            p = jnp.exp2(s - bound)
            pv = jnp.dot(p.astype(BF16), v, preferred_element_type=F32)
            if first:
                l_ref[i, rows, :] = _lane_partial_sum(p)
                acc_ref[i, rows, :] = pv
            else:
                l_ref[i, rows, :] += _lane_partial_sum(p)
                acc_ref[i, rows, :] += pv

    def update_online(j, rows, masked, first=False):
        for i in range(2):
            s, v = scores(j, i, rows, masked)
            m_cur = jnp.max(s, axis=1, keepdims=True)
            if first:
                m_new = jnp.broadcast_to(m_cur, (s.shape[0], LANES))
                p = jnp.exp2(s - m_cur)
                l_ref[i, rows, :] = _lane_partial_sum(p)
                acc_ref[i, rows, :] = jnp.dot(p.astype(BF16), v, preferred_element_type=F32)
            else:
                m_prev = m_ref[i, rows, :]
                m_new = jnp.maximum(m_prev, m_cur)
                alpha = jnp.exp2(m_prev - m_new)
                p = jnp.exp2(s - m_new[:, :1])
                l_ref[i, rows, :] = alpha * l_ref[i, rows, :] + _lane_partial_sum(p)
                acc_ref[i, rows, :] = alpha[:, :1] * acc_ref[i, rows, :] + jnp.dot(
                    p.astype(BF16), v, preferred_element_type=F32)
            m_ref[i, rows, :] = m_new

    def run(update, unroll):
        blocks_per_step = tq // tk
        all_rows = slice(0, tq)

        for r in range(blocks_per_step):
            update(blocks_per_step * qi + r, slice(r * tk, tq), masked=True, first=(r == 0))

        def body(t, carry):
            for u in range(unroll):
                update(unroll * t + u, all_rows, masked=False)
            return carry
        lax.fori_loop(0, (blocks_per_step // unroll) * qi, body, 0)

    @pl.when(bound <= ATT_BOUND_MAX)
    def _():
        run(update_bounded, ATT_UNROLL)

    @pl.when(bound > ATT_BOUND_MAX)
    def _():
        run(update_online, 1)

    lam = (jnp.exp(jnp.sum(lq1_ref[...] * lk1_ref[...], axis=1, keepdims=True))
           - jnp.exp(jnp.sum(lq2_ref[...] * lk2_ref[...], axis=1, keepdims=True))
           + lambda_init)
    l0 = jnp.sum(l_ref[0], axis=1, keepdims=True)
    l1 = jnp.sum(l_ref[1], axis=1, keepdims=True)
    o = acc_ref[0] / l0 - lam * (acc_ref[1] / l1)
    o = _rms_rows(o) * subln_ref[...] * (1.0 - lambda_init)
    o_ref[...] = o.astype(o_ref.dtype)


def _diff_attention(qk, v, score_bound, lq1, lk1, lq2, lk2, subln, lambda_init):
    s = qk.shape[0]
    hd = 2 * DIFF_HEAD_DIM
    assert ATT_TQ % ATT_TK == 0
    vec = lambda a: a.reshape(1, -1).astype(F32)
    small = lambda w: pl.BlockSpec((1, w), lambda h, i: (0, 0))
    kernel = functools.partial(_attn_kernel, lambda_init=lambda_init)
    return pl.pallas_call(
        kernel,
        grid=(DIFF_HEADS, s // ATT_TQ),
        in_specs=[pl.BlockSpec(memory_space=pltpu.SMEM),
                  pl.BlockSpec((ATT_TQ, hd), lambda h, i: (i, h)),
                  pl.BlockSpec((s, hd), lambda h, i: (0, DIFF_HEADS + h)),
                  pl.BlockSpec((s, hd), lambda h, i: (0, h)),
                  small(DIFF_HEAD_DIM), small(DIFF_HEAD_DIM),
                  small(DIFF_HEAD_DIM), small(DIFF_HEAD_DIM), small(hd)],
        out_specs=pl.BlockSpec((ATT_TQ, hd), lambda h, i: (i, h)),
        out_shape=jax.ShapeDtypeStruct((s, DIFF_HEADS * hd), BF16),
        scratch_shapes=[pltpu.VMEM((2, ATT_TQ, LANES), F32),
                        pltpu.VMEM((2, ATT_TQ, LANES), F32),
                        pltpu.VMEM((2, ATT_TQ, hd), F32)],
        compiler_params=_compiler_params(("parallel", "arbitrary")),
        name="diff_attention",
    )(score_bound, qk, qk, v, vec(lq1), vec(lk1), vec(lq2), vec(lk2), vec(subln))


def _out_kernel(a_ref, w_ref, x_ref, g_ref, xo_ref, xn_ref, wb_ref):
    @pl.when(pl.program_id(0) == 0)
    def _():
        wb_ref[...] = w_ref[...].astype(BF16)

    x = x_ref[...] + jnp.dot(a_ref[...], wb_ref[...], preferred_element_type=F32)
    xo_ref[...] = x
    xn_ref[...] = (_rms_rows(x) * g_ref[...]).astype(xn_ref.dtype)


def _out_project(a, w, layer, x, g):
    s, d = x.shape
    k = a.shape[1]
    row = lambda i: (i, 0)
    return pl.pallas_call(
        _out_kernel,
        grid=(s // OUT_TM,),
        in_specs=[pl.BlockSpec((OUT_TM, k), row),
                  pl.BlockSpec((None, k, d), lambda i: (layer, 0, 0),
                               pipeline_mode=pl.Buffered(1)),
                  pl.BlockSpec((OUT_TM, d), row),
                  pl.BlockSpec((1, d), lambda i: (0, 0))],
        out_specs=[pl.BlockSpec((OUT_TM, d), row), pl.BlockSpec((OUT_TM, d), row)],
        out_shape=[jax.ShapeDtypeStruct((s, d), F32), jax.ShapeDtypeStruct((s, d), BF16)],
        scratch_shapes=[pltpu.VMEM((k, d), BF16)],
        compiler_params=_compiler_params(("arbitrary",)),
        name="out_proj_residual_norm",
    )(a, w, x, g.reshape(1, d))


def _ffn_kernel(xn_ref, wg_ref, wu_ref, wd_ref, x_ref, *rest):
    g_ref, xo_ref, xn_out_ref = rest if len(rest) == 3 else (None, rest[0], None)
    f = pl.program_id(1)

    @pl.when(f == 0)
    def _():
        xo_ref[...] = x_ref[...]

    xn = xn_ref[...]
    gate = jnp.dot(xn, wg_ref[...].astype(BF16), preferred_element_type=F32)
    up = jnp.dot(xn, wu_ref[...].astype(BF16), preferred_element_type=F32)
    act = (_silu(gate) * up).astype(BF16)
    xo_ref[...] += jnp.dot(act, wd_ref[...].astype(BF16), preferred_element_type=F32)

    if xn_out_ref is not None:
        @pl.when(f == pl.num_programs(1) - 1)
        def _():
            xn_out_ref[...] = (_rms_rows(xo_ref[...]) * g_ref[...]).astype(xn_out_ref.dtype)


def _ffn(xn, wg, wu, wd, layer, x, g):
    s, d = x.shape
    hidden = wg.shape[2]
    row_block = lambda **kw: pl.BlockSpec((FFN_TM, d), lambda i, f: (i, 0), **kw)
    in_specs = [row_block(),
                pl.BlockSpec((None, d, FFN_TF), lambda i, f: (layer, 0, f)),
                pl.BlockSpec((None, d, FFN_TF), lambda i, f: (layer, 0, f)),
                pl.BlockSpec((None, FFN_TF, d), lambda i, f: (layer, f, 0)),
                row_block()]
    out_specs = [row_block()]
    out_shape = [jax.ShapeDtypeStruct((s, d), F32)]
    args = [xn, wg, wu, wd, x]
    if g is not None:
        in_specs.append(pl.BlockSpec((1, d), lambda i, f: (0, 0)))
        out_specs.append(row_block())
        out_shape.append(jax.ShapeDtypeStruct((s, d), BF16))
        args.append(g.reshape(1, d))
    outs = pl.pallas_call(
        _ffn_kernel,
        grid=(s // FFN_TM, hidden // FFN_TF),
        in_specs=in_specs,
        out_specs=out_specs,
        out_shape=out_shape,
        compiler_params=_compiler_params(("parallel", "arbitrary")),
        name="swiglu_ffn_residual_norm" if g is not None else "swiglu_ffn_residual",
    )(*args)
    return (outs[0], outs[1]) if g is not None else (outs[0], None)


def _rope_tables(seq, dim):
    inv = np.float32(ROPE_THETA) ** (-np.arange(0, dim, 2, dtype=np.float32) / np.float32(dim))
    ang = np.arange(seq, dtype=np.float32)[:, None] * inv[None, :]
    cos = np.cos(ang.astype(np.float64)).astype(np.float32)
    sin = np.sin(ang.astype(np.float64)).astype(np.float32)
    reps = LANES // dim
    cos_l = np.tile(np.concatenate([cos, cos], axis=1), (1, reps))
    sin_l = np.tile(np.concatenate([-sin, sin], axis=1), (1, reps))
    return jnp.asarray(cos_l), jnp.asarray(sin_l)


def kernel(x, norm_mix, norm_ffn, hyb_w_in, hyb_conv_w, hyb_w_out, diff_w_qkv, diff_q_norm,
           diff_k_norm, diff_lambda_q1, diff_lambda_k1, diff_lambda_q2, diff_lambda_k2,
           diff_subln, diff_w_out, ffn_w_gate, ffn_w_up, ffn_w_down):
    b, s, d = x.shape
    assert b == 1 and d == D_MODEL
    xs = x.reshape(s, d)
    cos_r, sin_r = _rope_tables(s, RET_DK)
    cos_a, sin_a = _rope_tables(s, DIFF_HEAD_DIM)

    xn = _rmsnorm(xs, norm_mix[0])
    for layer in range(DEPTH):
        j = layer // 2
        if layer % 2 == 0:
            qk_cols = 2 * RET_QK
            qk = _project_rope(xn, hyb_w_in, j, qk_cols, cos_r, sin_r, RET_DK // 2, None)
            rest = _project(xn, hyb_w_in, j, qk_cols, HYB_IN - qk_cols)
            mixed = _hybrid_mix(qk, rest, hyb_conv_w[j])
            w_out = hyb_w_out
        else:
            lambda_init = 0.8 - 0.6 * math.exp(-0.3 * layer)
            n_groups = DIFF_QK_COLS // DIFF_HEAD_DIM
            q_gain = diff_q_norm[j] * (LOG2_E * DIFF_HEAD_DIM ** -0.5)
            gain = jnp.concatenate([jnp.tile(q_gain, n_groups),
                                    jnp.tile(diff_k_norm[j], n_groups)]).reshape(1, -1)
            qk_cols = 2 * DIFF_QK_COLS
            qk = _project_rope(xn, diff_w_qkv, j, qk_cols, cos_a, sin_a,
                               DIFF_HEAD_DIM // 2, gain)
            v = _project(xn, diff_w_qkv, j, qk_cols, DIFF_QKV - qk_cols)
            score_bound = (ATT_BOUND_SLACK * DIFF_HEAD_DIM * jnp.max(jnp.abs(q_gain))
                           * jnp.max(jnp.abs(diff_k_norm[j]))).reshape(1).astype(F32)
            mixed = _diff_attention(qk, v, score_bound, diff_lambda_q1[j], diff_lambda_k1[j],
                                    diff_lambda_q2[j], diff_lambda_k2[j], diff_subln[j],
                                    lambda_init)
            w_out = diff_w_out
        xs, xn = _out_project(mixed, w_out, j, xs, norm_ffn[layer])
        g_next = norm_mix[layer + 1] if layer + 1 < DEPTH else None
        xs, xn = _ffn(xn, ffn_w_gate, ffn_w_up, ffn_w_down, layer, xs, g_next)
    return xs.reshape(b, s, d)
```

```python
import functools
import math

import jax
import jax.numpy as jnp
import numpy as np
from jax import lax
from jax.experimental import pallas as pl
from jax.experimental.pallas import tpu as pltpu

D_MODEL = 2048
DEPTH = 4
ROPE_THETA = 10000.0
NORM_EPS = 1e-6
RET_HEADS = 8
RET_DK = 64
RET_DV = 128
CONV_WIDTH = 1024
CONV_K = 3
DIFF_HEADS = 8
DIFF_HEAD_DIM = 128
FFN_HIDDEN = 5632
RET_QK = RET_HEADS * RET_DK
RET_V = RET_HEADS * RET_DV
HYB_IN = 2 * RET_QK + 2 * RET_V + 3 * CONV_WIDTH
DIFF_QKV = 6144
DIFF_QK_COLS = 2 * DIFF_HEADS * DIFF_HEAD_DIM

LANES = 128
SUBLANES = 8
VMEM_LIMIT_BYTES = 62 * 1024 * 1024

NORM_TM = 512
PROJ_TM = 2048
PROJ_TN = 512
PROJ_PLAIN_TN = 1024
OUT_TM = 512
FFN_TM = 1024
FFN_TF = 256
RET_TS = 256
ATT_TQ = 2048
ATT_TK = 512
ATT_UNROLL = 4
MASK_VALUE = -1e30
LOG2_E = math.log2(math.e)
ATT_BOUND_SLACK = 1.02
ATT_BOUND_MAX = 60.0

F32 = jnp.float32
BF16 = jnp.bfloat16


def _compiler_params(semantics):
    return pltpu.CompilerParams(dimension_semantics=semantics,
                                vmem_limit_bytes=VMEM_LIMIT_BYTES)


def _rms_rows(v):
    return v * lax.rsqrt(jnp.mean(v * v, axis=-1, keepdims=True) + NORM_EPS)


def _norm_kernel(x_ref, g_ref, o_ref):
    o_ref[...] = (_rms_rows(x_ref[...]) * g_ref[...]).astype(o_ref.dtype)


def _rmsnorm(x, g):
    s, d = x.shape
    return pl.pallas_call(
        _norm_kernel,
        grid=(s // NORM_TM,),
        in_specs=[pl.BlockSpec((NORM_TM, d), lambda i: (i, 0)),
                  pl.BlockSpec((1, d), lambda i: (0, 0))],
        out_specs=pl.BlockSpec((NORM_TM, d), lambda i: (i, 0)),
        out_shape=jax.ShapeDtypeStruct((s, d), BF16),
        compiler_params=_compiler_params(("parallel",)),
        name="rmsnorm",
    )(x, g.reshape(1, d))


def _rotate_half(blk, half):
    if 2 * half == LANES:
        return pltpu.roll(blk, half, axis=1)
    lane = lax.broadcasted_iota(jnp.int32, blk.shape, 1)
    first = (lane % (2 * half)) < half
    return jnp.where(first, pltpu.roll(blk, LANES - half, axis=1),
                     pltpu.roll(blk, half, axis=1))


def _proj_plain_kernel(x_ref, w_ref, o_ref):
    o_ref[...] = jnp.dot(x_ref[...], w_ref[...].astype(BF16),
                         preferred_element_type=F32).astype(o_ref.dtype)


def _proj_rope_kernel(x_ref, w_ref, cos_ref, sin_ref, *rest, rope_half):
    gain_ref, o_ref, acc_even_ref, acc_odd_ref = rest if len(rest) == 4 else (None,) + rest
    t = pl.program_id(0)
    n_groups = acc_even_ref.shape[1] // LANES
    k_chunk = x_ref.shape[1] // n_groups

    @pl.when(t == 0)
    def _():
        acc_odd_ref[...] = jnp.zeros_like(acc_odd_ref)

    def run(cur_ref, prev_ref):
        cur_ref[...] = jnp.dot(x_ref[...], w_ref[...].astype(BF16),
                               preferred_element_type=F32)
        for g in range(n_groups):
            cols = slice(g * LANES, (g + 1) * LANES)
            blk = prev_ref[:, cols]
            if gain_ref is not None:
                blk = _rms_rows(blk) * gain_ref[:, cols]
            out = blk * cos_ref[...] + _rotate_half(blk, rope_half) * sin_ref[...]
            o_ref[:, cols] = out.astype(o_ref.dtype)

    @pl.when(t % 2 == 0)
    def _():
        run(acc_even_ref, acc_odd_ref)

    @pl.when(t % 2 == 1)
    def _():
        run(acc_odd_ref, acc_even_ref)


def _project_rope(xn, w, layer, n_cols, cos, sin, rope_half, gain):
    s, d = xn.shape
    n_tiles = n_cols // PROJ_TN
    last = (s // PROJ_TM) * n_tiles - 1
    mm_row = lambda t: jnp.minimum(t, last) // n_tiles
    mm_col = lambda t: jnp.minimum(t, last) % n_tiles
    ep_row = lambda t: jnp.maximum(t - 1, 0) // n_tiles
    ep_col = lambda t: jnp.maximum(t - 1, 0) % n_tiles
    in_specs = [pl.BlockSpec((PROJ_TM, d), lambda t: (mm_row(t), 0)),
                pl.BlockSpec((None, d, PROJ_TN), lambda t: (layer, 0, mm_col(t))),
                pl.BlockSpec((PROJ_TM, LANES), lambda t: (ep_row(t), 0)),
                pl.BlockSpec((PROJ_TM, LANES), lambda t: (ep_row(t), 0))]
    args = [xn, w, cos, sin]
    if gain is not None:
        in_specs.append(pl.BlockSpec((1, PROJ_TN), lambda t: (0, ep_col(t))))
        args.append(gain)
    return pl.pallas_call(
        functools.partial(_proj_rope_kernel, rope_half=rope_half),
        grid=(last + 2,),
        in_specs=in_specs,
        out_specs=pl.BlockSpec((PROJ_TM, PROJ_TN), lambda t: (ep_row(t), ep_col(t))),
        out_shape=jax.ShapeDtypeStruct((s, n_cols), BF16),
        scratch_shapes=[pltpu.VMEM((PROJ_TM, PROJ_TN), F32),
                        pltpu.VMEM((PROJ_TM, PROJ_TN), F32)],
        compiler_params=_compiler_params(("arbitrary",)),
        name="proj_rope" if gain is None else "proj_qknorm_rope",
    )(*args)


def _project(xn, w, layer, col_start, n_cols):
    s, d = xn.shape
    tn = PROJ_PLAIN_TN
    first = col_start // tn
    return pl.pallas_call(
        _proj_plain_kernel,
        grid=(s // PROJ_TM, n_cols // tn),
        in_specs=[pl.BlockSpec((PROJ_TM, d), lambda m, j: (m, 0)),
                  pl.BlockSpec((None, d, tn), lambda m, j: (layer, 0, first + j))],
        out_specs=pl.BlockSpec((PROJ_TM, tn), lambda m, j: (m, j)),
        out_shape=jax.ShapeDtypeStruct((s, n_cols), BF16),
        compiler_params=_compiler_params(("parallel", "arbitrary")),
        name="proj_plain",
    )(xn, w)


def _retention_tables(ts):
    h = np.arange(RET_HEADS, dtype=np.float64)
    log_g = np.log(1.0 - np.exp2(-5.0 - h))
    idx = np.arange(ts, dtype=np.float64)
    dist = idx[:, None] - idx[None, :]
    scale = RET_DK ** -0.5
    decay = np.where(dist >= 0, np.exp(log_g[:, None, None] * np.maximum(dist, 0.0)), 0.0) * scale
    q_dec = np.exp(log_g[:, None] * (idx + 1.0))
    k_dec = np.exp(log_g[:, None] * (ts - 1.0 - idx)) * scale
    chunk_dec = np.exp(log_g * ts)
    q_dec = np.broadcast_to(q_dec[:, :, None], (RET_HEADS, ts, LANES))
    k_dec = np.broadcast_to(k_dec[:, :, None], (RET_HEADS, ts, LANES))
    return (jnp.asarray(decay, F32), jnp.asarray(q_dec, F32), jnp.asarray(k_dec, F32),
            [float(np.float32(c)) for c in chunk_dec])


def _silu(v):
    return v / (1.0 + jnp.exp(-v))


def _hybrid_kernel(q_ref, k_ref, v_ref, g_ref, cb_ref, cc_ref, cx_ref, convw_ref,
                   decay_ref, qdec_ref, kdec_ref, o_ref, state_ref, u_ref, *, chunk_dec):
    ts = q_ref.shape[0]

    @pl.when(pl.program_id(0) == 0)
    def _():
        state_ref[...] = jnp.zeros_like(state_ref)
        u_ref[0:SUBLANES, :] = jnp.zeros((SUBLANES, u_ref.shape[1]), F32)

    lane = lax.broadcasted_iota(jnp.int32, (ts, LANES), 1)
    for h in range(RET_HEADS):
        pair = slice((h // 2) * LANES, (h // 2 + 1) * LANES)
        head = slice(h * RET_DV, (h + 1) * RET_DV)
        in_head = (lane // RET_DK) == (h % 2)
        qm = jnp.where(in_head, q_ref[:, pair].astype(F32), 0.0)
        kp = k_ref[:, pair]
        vh = v_ref[:, head]
        s = lax.dot_general(qm.astype(BF16), kp, (((1,), (1,)), ((), ())),
                            preferred_element_type=F32)
        inner = (s * decay_ref[h]).astype(BF16)
        qd = (qm * qdec_ref[h]).astype(BF16)
        o = (jnp.dot(inner, vh, preferred_element_type=F32)
             + jnp.dot(qd, state_ref[h].astype(BF16), preferred_element_type=F32))
        kd = (kp.astype(F32) * kdec_ref[h]).astype(BF16)
        state_ref[h] = chunk_dec[h] * state_ref[h] + lax.dot_general(
            kd, vh, (((0,), (0,)), ((), ())), preferred_element_type=F32)
        gate = _silu(g_ref[:, head].astype(F32))
        o_ref[:, head] = (gate * _rms_rows(o)).astype(o_ref.dtype)

    u_ref[SUBLANES:SUBLANES + ts, :] = cc_ref[...].astype(F32) * cx_ref[...].astype(F32)
    y = (convw_ref[0:1, :] * u_ref[SUBLANES - 2:SUBLANES - 2 + ts, :]
         + convw_ref[1:2, :] * u_ref[SUBLANES - 1:SUBLANES - 1 + ts, :]
         + convw_ref[2:3, :] * u_ref[SUBLANES:SUBLANES + ts, :])
    o_ref[:, RET_V:] = (cb_ref[...].astype(F32) * y).astype(o_ref.dtype)
    u_ref[0:SUBLANES, :] = u_ref[ts:ts + SUBLANES, :]


def _hybrid_mix(qk, rest, conv_w):
    s = qk.shape[0]
    ts = RET_TS
    decay, q_dec, k_dec, chunk_dec = _retention_tables(ts)
    qk_blk = RET_QK
    w_blk = RET_V
    const3 = lambda i: (0, 0, 0)
    kernel = functools.partial(_hybrid_kernel, chunk_dec=chunk_dec)
    return pl.pallas_call(
        kernel,
        grid=(s // ts,),
        in_specs=[pl.BlockSpec((ts, qk_blk), lambda i: (i, 0)),
                  pl.BlockSpec((ts, qk_blk), lambda i: (i, 1)),
                  pl.BlockSpec((ts, w_blk), lambda i: (i, 0)),
                  pl.BlockSpec((ts, w_blk), lambda i: (i, 1)),
                  pl.BlockSpec((ts, w_blk), lambda i: (i, 2)),
                  pl.BlockSpec((ts, w_blk), lambda i: (i, 3)),
                  pl.BlockSpec((ts, w_blk), lambda i: (i, 4)),
                  pl.BlockSpec((SUBLANES, CONV_WIDTH), lambda i: (0, 0)),
                  pl.BlockSpec((RET_HEADS, ts, ts), const3),
                  pl.BlockSpec((RET_HEADS, ts, LANES), const3),
                  pl.BlockSpec((RET_HEADS, ts, LANES), const3)],
        out_specs=pl.BlockSpec((ts, RET_V + CONV_WIDTH), lambda i: (i, 0)),
        out_shape=jax.ShapeDtypeStruct((s, RET_V + CONV_WIDTH), BF16),
        scratch_shapes=[pltpu.VMEM((RET_HEADS, LANES, RET_DV), F32),
                        pltpu.VMEM((SUBLANES + ts, CONV_WIDTH), F32)],
        compiler_params=_compiler_params(("arbitrary",)),
        name="retention_conv",
    )(qk, qk, rest, rest, rest, rest, rest,
      jnp.pad(conv_w, ((0, SUBLANES - CONV_K), (0, 0))), decay, q_dec, k_dec)


def _lane_partial_sum(p):
    total = p[:, 0:LANES]
    for t in range(1, p.shape[1] // LANES):
        total = total + p[:, t * LANES:(t + 1) * LANES]
    return total


def _attn_kernel(bound_ref, q_ref, k_ref, v_ref, lq1_ref, lk1_ref, lq2_ref, lk2_ref,
                 subln_ref, o_ref, m_ref, l_ref, acc_ref, *, lambda_init):
    tq = q_ref.shape[0]
    tk = ATT_TK
    d = DIFF_HEAD_DIM
    qi = pl.program_id(1)
    bound = bound_ref[0]

    def scores(j, i, rows, masked):
        start = pl.multiple_of(j * tk, tk)
        s = lax.dot_general(q_ref[rows, i * d:(i + 1) * d],
                            k_ref[pl.ds(start, tk), i * d:(i + 1) * d],
                            (((1,), (1,)), ((), ())), preferred_element_type=F32)
        if masked:
            row = lax.broadcasted_iota(jnp.int32, s.shape, 0)
            col = lax.broadcasted_iota(jnp.int32, s.shape, 1)
            s = jnp.where(col <= row, s, MASK_VALUE)
        return s, v_ref[pl.ds(start, tk), :]

    def update_bounded(j, rows, masked, first=False):
        for i in range(2):
            s, v = scores(j, i, rows, masked)
            p = jnp.exp2(s - bound)
            pv = jnp.dot(p.astype(BF16), v, preferred_element_type=F32)
            if first:
                l_ref[i, rows, :] = _lane_partial_sum(p)
                acc_ref[i, rows, :] = pv
            else:
                l_ref[i, rows, :] += _lane_partial_sum(p)
                acc_ref[i, rows, :] += pv

    def update_online(j, rows, masked, first=False):
        for i in range(2):
            s, v = scores(j, i, rows, masked)
            m_cur = jnp.max(s, axis=1, keepdims=True)
            if first:
                m_new = jnp.broadcast_to(m_cur, (s.shape[0], LANES))
                p = jnp.exp2(s - m_cur)
                l_ref[i, rows, :] = _lane_partial_sum(p)
                acc_ref[i, rows, :] = jnp.dot(p.astype(BF16), v, preferred_element_type=F32)
            else:
                m_prev = m_ref[i, rows, :]
                m_new = jnp.maximum(m_prev, m_cur)
                alpha = jnp.exp2(m_prev - m_new)
                p = jnp.exp2(s - m_new[:, :1])
                l_ref[i, rows, :] = alpha * l_ref[i, rows, :] + _lane_partial_sum(p)
                acc_ref[i, rows, :] = alpha[:, :1] * acc_ref[i, rows, :] + jnp.dot(
                    p.astype(BF16), v, preferred_element_type=F32)
            m_ref[i, rows, :] = m_new

    def run(update, unroll):
        blocks_per_step = tq // tk
        all_rows = slice(0, tq)

        for r in range(blocks_per_step):
            update(blocks_per_step * qi + r, slice(r * tk, tq), masked=True, first=(r == 0))

        def body(t, carry):
            for u in range(unroll):
                update(unroll * t + u, all_rows, masked=False)
            return carry
        lax.fori_loop(0, (blocks_per_step // unroll) * qi, body, 0)

    @pl.when(bound <= ATT_BOUND_MAX)
    def _():
        run(update_bounded, ATT_UNROLL)

    @pl.when(bound > ATT_BOUND_MAX)
    def _():
        run(update_online, 1)

    lam = (jnp.exp(jnp.sum(lq1_ref[...] * lk1_ref[...], axis=1, keepdims=True))
           - jnp.exp(jnp.sum(lq2_ref[...] * lk2_ref[...], axis=1, keepdims=True))
           + lambda_init)
    l0 = jnp.sum(l_ref[0], axis=1, keepdims=True)
    l1 = jnp.sum(l_ref[1], axis=1, keepdims=True)
    o = acc_ref[0] * (1.0 / l0) - acc_ref[1] * (lam / l1)
    o = _rms_rows(o) * (subln_ref[...] * (1.0 - lambda_init))
    o_ref[...] = o.astype(o_ref.dtype)


def _diff_attention(qk, v, score_bound, lq1, lk1, lq2, lk2, subln, lambda_init):
    s = qk.shape[0]
    hd = 2 * DIFF_HEAD_DIM
    assert ATT_TQ % ATT_TK == 0
    vec = lambda a: a.reshape(1, -1).astype(F32)
    small = lambda w: pl.BlockSpec((1, w), lambda h, i: (0, 0))
    kernel = functools.partial(_attn_kernel, lambda_init=lambda_init)
    return pl.pallas_call(
        kernel,
        grid=(DIFF_HEADS, s // ATT_TQ),
        in_specs=[pl.BlockSpec(memory_space=pltpu.SMEM),
                  pl.BlockSpec((ATT_TQ, hd), lambda h, i: (i, h)),
                  pl.BlockSpec((s, hd), lambda h, i: (0, DIFF_HEADS + h)),
                  pl.BlockSpec((s, hd), lambda h, i: (0, h)),
                  small(DIFF_HEAD_DIM), small(DIFF_HEAD_DIM),
                  small(DIFF_HEAD_DIM), small(DIFF_HEAD_DIM), small(hd)],
        out_specs=pl.BlockSpec((ATT_TQ, hd), lambda h, i: (i, h)),
        out_shape=jax.ShapeDtypeStruct((s, DIFF_HEADS * hd), BF16),
        scratch_shapes=[pltpu.VMEM((2, ATT_TQ, LANES), F32),
                        pltpu.VMEM((2, ATT_TQ, LANES), F32),
                        pltpu.VMEM((2, ATT_TQ, hd), F32)],
        compiler_params=_compiler_params(("parallel", "arbitrary")),
        name="diff_attention",
    )(score_bound, qk, qk, v, vec(lq1), vec(lk1), vec(lq2), vec(lk2), vec(subln))


def _out_kernel(a_ref, w_ref, x_ref, g_ref, xo_ref, xn_ref, wb_ref):
    @pl.when(pl.program_id(0) == 0)
    def _():
        wb_ref[...] = w_ref[...].astype(BF16)

    x = x_ref[...] + jnp.dot(a_ref[...], wb_ref[...], preferred_element_type=F32)
    xo_ref[...] = x
    xn_ref[...] = (_rms_rows(x) * g_ref[...]).astype(xn_ref.dtype)


def _out_project(a, w, layer, x, g):
    s, d = x.shape
    k = a.shape[1]
    row = lambda i: (i, 0)
    return pl.pallas_call(
        _out_kernel,
        grid=(s // OUT_TM,),
        in_specs=[pl.BlockSpec((OUT_TM, k), row),
                  pl.BlockSpec((None, k, d), lambda i: (layer, 0, 0),
                               pipeline_mode=pl.Buffered(1)),
                  pl.BlockSpec((OUT_TM, d), row),
                  pl.BlockSpec((1, d), lambda i: (0, 0))],
        out_specs=[pl.BlockSpec((OUT_TM, d), row), pl.BlockSpec((OUT_TM, d), row)],
        out_shape=[jax.ShapeDtypeStruct((s, d), F32), jax.ShapeDtypeStruct((s, d), BF16)],
        scratch_shapes=[pltpu.VMEM((k, d), BF16)],
        compiler_params=_compiler_params(("arbitrary",)),
        name="out_proj_residual_norm",
    )(a, w, x, g.reshape(1, d))


def _ffn_kernel(xn_ref, wg_ref, wu_ref, wd_ref, x_ref, *rest):
    g_ref, xo_ref, xn_out_ref = rest if len(rest) == 3 else (None, rest[0], None)
    f = pl.program_id(1)

    def step(base_ref):
        xn = xn_ref[...]
        gate = jnp.dot(xn, wg_ref[...].astype(BF16), preferred_element_type=F32)
        up = jnp.dot(xn, wu_ref[...].astype(BF16), preferred_element_type=F32)
        act = (_silu(gate) * up).astype(BF16)
        xo_ref[...] = base_ref[...] + jnp.dot(act, wd_ref[...].astype(BF16),
                                              preferred_element_type=F32)

    @pl.when(f == 0)
    def _():
        step(x_ref)

    @pl.when(f > 0)
    def _():
        step(xo_ref)

    if xn_out_ref is not None:
        @pl.when(f == pl.num_programs(1) - 1)
        def _():
            xn_out_ref[...] = (_rms_rows(xo_ref[...]) * g_ref[...]).astype(xn_out_ref.dtype)


def _ffn(xn, wg, wu, wd, layer, x, g):
    s, d = x.shape
    hidden = wg.shape[2]
    row_block = lambda **kw: pl.BlockSpec((FFN_TM, d), lambda i, f: (i, 0), **kw)
    in_specs = [row_block(),
                pl.BlockSpec((None, d, FFN_TF), lambda i, f: (layer, 0, f)),
                pl.BlockSpec((None, d, FFN_TF), lambda i, f: (layer, 0, f)),
                pl.BlockSpec((None, FFN_TF, d), lambda i, f: (layer, f, 0)),
                row_block()]
    out_specs = [row_block()]
    out_shape = [jax.ShapeDtypeStruct((s, d), F32)]
    args = [xn, wg, wu, wd, x]
    if g is not None:
        in_specs.append(pl.BlockSpec((1, d), lambda i, f: (0, 0)))
        out_specs.append(row_block())
        out_shape.append(jax.ShapeDtypeStruct((s, d), BF16))
        args.append(g.reshape(1, d))
    outs = pl.pallas_call(
        _ffn_kernel,
        grid=(s // FFN_TM, hidden // FFN_TF),
        in_specs=in_specs,
        out_specs=out_specs,
        out_shape=out_shape,
        compiler_params=_compiler_params(("parallel", "arbitrary")),
        name="swiglu_ffn_residual_norm" if g is not None else "swiglu_ffn_residual",
    )(*args)
    return (outs[0], outs[1]) if g is not None else (outs[0], None)


def _rope_tables(seq, dim):
    inv = np.float32(ROPE_THETA) ** (-np.arange(0, dim, 2, dtype=np.float32) / np.float32(dim))
    ang = np.arange(seq, dtype=np.float32)[:, None] * inv[None, :]
    cos = np.cos(ang.astype(np.float64)).astype(np.float32)
    sin = np.sin(ang.astype(np.float64)).astype(np.float32)
    reps = LANES // dim
    cos_l = np.tile(np.concatenate([cos, cos], axis=1), (1, reps))
    sin_l = np.tile(np.concatenate([-sin, sin], axis=1), (1, reps))
    return jnp.asarray(cos_l), jnp.asarray(sin_l)


def kernel(x, norm_mix, norm_ffn, hyb_w_in, hyb_conv_w, hyb_w_out, diff_w_qkv, diff_q_norm,
           diff_k_norm, diff_lambda_q1, diff_lambda_k1, diff_lambda_q2, diff_lambda_k2,
           diff_subln, diff_w_out, ffn_w_gate, ffn_w_up, ffn_w_down):
    b, s, d = x.shape
    assert b == 1 and d == D_MODEL
    xs = x.reshape(s, d)
    cos_r, sin_r = _rope_tables(s, RET_DK)
    cos_a, sin_a = _rope_tables(s, DIFF_HEAD_DIM)

    xn = _rmsnorm(xs, norm_mix[0])
    for layer in range(DEPTH):
        j = layer // 2
        if layer % 2 == 0:
            qk_cols = 2 * RET_QK
            qk = _project_rope(xn, hyb_w_in, j, qk_cols, cos_r, sin_r, RET_DK // 2, None)
            rest = _project(xn, hyb_w_in, j, qk_cols, HYB_IN - qk_cols)
            mixed = _hybrid_mix(qk, rest, hyb_conv_w[j])
            w_out = hyb_w_out
        else:
            lambda_init = 0.8 - 0.6 * math.exp(-0.3 * layer)
            n_groups = DIFF_QK_COLS // DIFF_HEAD_DIM
            q_gain = diff_q_norm[j] * (LOG2_E * DIFF_HEAD_DIM ** -0.5)
            gain = jnp.concatenate([jnp.tile(q_gain, n_groups),
                                    jnp.tile(diff_k_norm[j], n_groups)]).reshape(1, -1)
            qk_cols = 2 * DIFF_QK_COLS
            qk = _project_rope(xn, diff_w_qkv, j, qk_cols, cos_a, sin_a,
                               DIFF_HEAD_DIM // 2, gain)
            v = _project(xn, diff_w_qkv, j, qk_cols, DIFF_QKV - qk_cols)
            score_bound = (ATT_BOUND_SLACK * DIFF_HEAD_DIM * jnp.max(jnp.abs(q_gain))
                           * jnp.max(jnp.abs(diff_k_norm[j]))).reshape(1).astype(F32)
            mixed = _diff_attention(qk, v, score_bound, diff_lambda_q1[j], diff_lambda_k1[j],
                                    diff_lambda_q2[j], diff_lambda_k2[j], diff_subln[j],
                                    lambda_init)
            w_out = diff_w_out
        xs, xn = _out_project(mixed, w_out, j, xs, norm_ffn[layer])
        g_next = norm_mix[layer + 1] if layer + 1 < DEPTH else None
        xs, xn = _ffn(xn, ffn_w_gate, ffn_w_up, ffn_w_down, layer, xs, g_next)
    return xs.reshape(b, s, d)
```

```python
import functools
import math

import jax
import jax.numpy as jnp
import numpy as np
from jax import lax
from jax.experimental import pallas as pl
from jax.experimental.pallas import tpu as pltpu

D_MODEL = 2048
DEPTH = 4
ROPE_THETA = 10000.0
NORM_EPS = 1e-6
RET_HEADS = 8
RET_DK = 64
RET_DV = 128
CONV_WIDTH = 1024
CONV_K = 3
DIFF_HEADS = 8
DIFF_HEAD_DIM = 128
FFN_HIDDEN = 5632
RET_QK = RET_HEADS * RET_DK
RET_V = RET_HEADS * RET_DV
HYB_IN = 2 * RET_QK + 2 * RET_V + 3 * CONV_WIDTH
DIFF_QKV = 6144
DIFF_QK_COLS = 2 * DIFF_HEADS * DIFF_HEAD_DIM

LANES = 128
SUBLANES = 8
VMEM_LIMIT_BYTES = 62 * 1024 * 1024

NORM_TM = 512
PROJ_TM = 2048
PROJ_TN = 512
PROJ_PLAIN_TN = 1024
OUT_TM = 512
FFN_TM = 1024
FFN_TF = 256
RET_TS = 256
ATT_TQ = 2048
ATT_TK = 512
ATT_UNROLL = 4
MASK_VALUE = -1e30
LOG2_E = math.log2(math.e)
ATT_BOUND_SLACK = 1.02
ATT_BOUND_MAX = 60.0

F32 = jnp.float32
BF16 = jnp.bfloat16


def _compiler_params(semantics):
    return pltpu.CompilerParams(dimension_semantics=semantics,
                                vmem_limit_bytes=VMEM_LIMIT_BYTES)


def _rms_rows(v):
    return v * lax.rsqrt(jnp.mean(v * v, axis=-1, keepdims=True) + NORM_EPS)


def _norm_kernel(x_ref, g_ref, o_ref):
    o_ref[...] = (_rms_rows(x_ref[...]) * g_ref[...]).astype(o_ref.dtype)


def _rmsnorm(x, g):
    s, d = x.shape
    return pl.pallas_call(
        _norm_kernel,
        grid=(s // NORM_TM,),
        in_specs=[pl.BlockSpec((NORM_TM, d), lambda i: (i, 0)),
                  pl.BlockSpec((1, d), lambda i: (0, 0))],
        out_specs=pl.BlockSpec((NORM_TM, d), lambda i: (i, 0)),
        out_shape=jax.ShapeDtypeStruct((s, d), BF16),
        compiler_params=_compiler_params(("parallel",)),
        name="rmsnorm",
    )(x, g.reshape(1, d))


def _rotate_half(blk, half):
    if 2 * half == LANES:
        return pltpu.roll(blk, half, axis=1)
    lane = lax.broadcasted_iota(jnp.int32, blk.shape, 1)
    first = (lane % (2 * half)) < half
    return jnp.where(first, pltpu.roll(blk, LANES - half, axis=1),
                     pltpu.roll(blk, half, axis=1))


def _proj_plain_kernel(x_ref, w_ref, o_ref):
    o_ref[...] = jnp.dot(x_ref[...], w_ref[...].astype(BF16),
                         preferred_element_type=F32).astype(o_ref.dtype)


def _proj_rope_kernel(x_ref, w_ref, cos_ref, sin_ref, *rest, rope_half):
    gain_ref, o_ref, acc_even_ref, acc_odd_ref = rest if len(rest) == 4 else (None,) + rest
    t = pl.program_id(0)
    n_groups = acc_even_ref.shape[1] // LANES
    k_chunk = x_ref.shape[1] // n_groups

    @pl.when(t == 0)
    def _():
        acc_odd_ref[...] = jnp.zeros_like(acc_odd_ref)

    def run(cur_ref, prev_ref):
        cur_ref[...] = jnp.dot(x_ref[...], w_ref[...].astype(BF16),
                               preferred_element_type=F32)
        for g in range(n_groups):
            cols = slice(g * LANES, (g + 1) * LANES)
            blk = prev_ref[:, cols]
            if gain_ref is not None:
                blk = _rms_rows(blk) * gain_ref[:, cols]
            out = blk * cos_ref[...] + _rotate_half(blk, rope_half) * sin_ref[...]
            o_ref[:, cols] = out.astype(o_ref.dtype)

    @pl.when(t % 2 == 0)
    def _():
        run(acc_even_ref, acc_odd_ref)

    @pl.when(t % 2 == 1)
    def _():
        run(acc_odd_ref, acc_even_ref)


def _project_rope(xn, w, layer, n_cols, cos, sin, rope_half, gain):
    s, d = xn.shape
    n_tiles = n_cols // PROJ_TN
    last = (s // PROJ_TM) * n_tiles - 1
    mm_row = lambda t: jnp.minimum(t, last) // n_tiles
    mm_col = lambda t: jnp.minimum(t, last) % n_tiles
    ep_row = lambda t: jnp.maximum(t - 1, 0) // n_tiles
    ep_col = lambda t: jnp.maximum(t - 1, 0) % n_tiles
    in_specs = [pl.BlockSpec((PROJ_TM, d), lambda t: (mm_row(t), 0)),
                pl.BlockSpec((None, d, PROJ_TN), lambda t: (layer, 0, mm_col(t))),
                pl.BlockSpec((PROJ_TM, LANES), lambda t: (ep_row(t), 0)),
                pl.BlockSpec((PROJ_TM, LANES), lambda t: (ep_row(t), 0))]
    args = [xn, w, cos, sin]
    if gain is not None:
        in_specs.append(pl.BlockSpec((1, PROJ_TN), lambda t: (0, ep_col(t))))
        args.append(gain)
    return pl.pallas_call(
        functools.partial(_proj_rope_kernel, rope_half=rope_half),
        grid=(last + 2,),
        in_specs=in_specs,
        out_specs=pl.BlockSpec((PROJ_TM, PROJ_TN), lambda t: (ep_row(t), ep_col(t))),
        out_shape=jax.ShapeDtypeStruct((s, n_cols), BF16),
        scratch_shapes=[pltpu.VMEM((PROJ_TM, PROJ_TN), F32),
                        pltpu.VMEM((PROJ_TM, PROJ_TN), F32)],
        compiler_params=_compiler_params(("arbitrary",)),
        name="proj_rope" if gain is None else "proj_qknorm_rope",
    )(*args)


def _project(xn, w, layer, col_start, n_cols):
    s, d = xn.shape
    tn = PROJ_PLAIN_TN
    first = col_start // tn
    return pl.pallas_call(
        _proj_plain_kernel,
        grid=(s // PROJ_TM, n_cols // tn),
        in_specs=[pl.BlockSpec((PROJ_TM, d), lambda m, j: (m, 0)),
                  pl.BlockSpec((None, d, tn), lambda m, j: (layer, 0, first + j))],
        out_specs=pl.BlockSpec((PROJ_TM, tn), lambda m, j: (m, j)),
        out_shape=jax.ShapeDtypeStruct((s, n_cols), BF16),
        compiler_params=_compiler_params(("parallel", "arbitrary")),
        name="proj_plain",
    )(xn, w)


def _retention_tables(ts):
    h = np.arange(RET_HEADS, dtype=np.float64)
    log_g = np.log(1.0 - np.exp2(-5.0 - h))
    idx = np.arange(ts, dtype=np.float64)
    dist = idx[:, None] - idx[None, :]
    scale = RET_DK ** -0.5
    decay = np.where(dist >= 0, np.exp(log_g[:, None, None] * np.maximum(dist, 0.0)), 0.0) * scale
    q_dec = np.exp(log_g[:, None] * (idx + 1.0))
    k_dec = np.exp(log_g[:, None] * (ts - 1.0 - idx)) * scale
    chunk_dec = np.exp(log_g * ts)
    q_dec = np.broadcast_to(q_dec[:, :, None], (RET_HEADS, ts, LANES))
    k_dec = np.broadcast_to(k_dec[:, :, None], (RET_HEADS, ts, LANES))
    return (jnp.asarray(decay, F32), jnp.asarray(q_dec, F32), jnp.asarray(k_dec, F32),
            [float(np.float32(c)) for c in chunk_dec])


def _silu(v):
    return v / (1.0 + jnp.exp(-v))


def _hybrid_kernel(q_ref, k_ref, v_ref, g_ref, cb_ref, cc_ref, cx_ref, convw_ref,
                   decay_ref, qdec_ref, kdec_ref, w_ref, x_ref, gain_ref,
                   xo_ref, xn_ref, state_ref, u_ref, wb_ref, o_ref, prev_ref, *, chunk_dec):
    ts = q_ref.shape[0]

    @pl.when(pl.program_id(0) == 0)
    def _():
        state_ref[...] = jnp.zeros_like(state_ref)
        u_ref[0:SUBLANES, :] = jnp.zeros((SUBLANES, u_ref.shape[1]), F32)
        wb_ref[...] = w_ref[...].astype(BF16)
        prev_ref[...] = jnp.zeros_like(prev_ref)

    x_new = x_ref[...] + jnp.dot(prev_ref[...], wb_ref[...], preferred_element_type=F32)
    xo_ref[...] = x_new
    xn_ref[...] = (_rms_rows(x_new) * gain_ref[...]).astype(xn_ref.dtype)

    lane = lax.broadcasted_iota(jnp.int32, (ts, LANES), 1)
    for h in range(RET_HEADS):
        pair = slice((h // 2) * LANES, (h // 2 + 1) * LANES)
        head = slice(h * RET_DV, (h + 1) * RET_DV)
        in_head = (lane // RET_DK) == (h % 2)
        qm = jnp.where(in_head, q_ref[:, pair].astype(F32), 0.0)
        kp = k_ref[:, pair]
        vh = v_ref[:, head]
        s = lax.dot_general(qm.astype(BF16), kp, (((1,), (1,)), ((), ())),
                            preferred_element_type=F32)
        inner = (s * decay_ref[h]).astype(BF16)
        qd = (qm * qdec_ref[h]).astype(BF16)
        o = (jnp.dot(inner, vh, preferred_element_type=F32)
             + jnp.dot(qd, state_ref[h].astype(BF16), preferred_element_type=F32))
        kd = (kp.astype(F32) * kdec_ref[h]).astype(BF16)
        state_ref[h] = chunk_dec[h] * state_ref[h] + lax.dot_general(
            kd, vh, (((0,), (0,)), ((), ())), preferred_element_type=F32)
        gate = _silu(g_ref[:, head].astype(F32))
        o_ref[:, head] = (gate * _rms_rows(o)).astype(o_ref.dtype)

    u_ref[SUBLANES:SUBLANES + ts, :] = cc_ref[...].astype(F32) * cx_ref[...].astype(F32)
    y = (convw_ref[0:1, :] * u_ref[SUBLANES - 2:SUBLANES - 2 + ts, :]
         + convw_ref[1:2, :] * u_ref[SUBLANES - 1:SUBLANES - 1 + ts, :]
         + convw_ref[2:3, :] * u_ref[SUBLANES:SUBLANES + ts, :])
    o_ref[:, RET_V:] = (cb_ref[...].astype(F32) * y).astype(o_ref.dtype)
    u_ref[0:SUBLANES, :] = u_ref[ts:ts + SUBLANES, :]
    prev_ref[...] = o_ref[...]


def _hybrid_mix(qk, rest, conv_w, w_out, layer, x, gain):
    s, d = x.shape
    ts = RET_TS
    last = s // ts - 1
    cur = lambda t: jnp.minimum(t, last)
    done = lambda t: jnp.maximum(t - 1, 0)
    decay, q_dec, k_dec, chunk_dec = _retention_tables(ts)
    qk_blk = RET_QK
    w_blk = RET_V
    const3 = lambda i: (0, 0, 0)
    kernel = functools.partial(_hybrid_kernel, chunk_dec=chunk_dec)
    return pl.pallas_call(
        kernel,
        grid=(last + 2,),
        in_specs=[pl.BlockSpec((ts, qk_blk), lambda t: (cur(t), 0)),
                  pl.BlockSpec((ts, qk_blk), lambda t: (cur(t), 1)),
                  pl.BlockSpec((ts, w_blk), lambda t: (cur(t), 0)),
                  pl.BlockSpec((ts, w_blk), lambda t: (cur(t), 1)),
                  pl.BlockSpec((ts, w_blk), lambda t: (cur(t), 2)),
                  pl.BlockSpec((ts, w_blk), lambda t: (cur(t), 3)),
                  pl.BlockSpec((ts, w_blk), lambda t: (cur(t), 4)),
                  pl.BlockSpec((SUBLANES, CONV_WIDTH), lambda t: (0, 0)),
                  pl.BlockSpec((RET_HEADS, ts, ts), const3),
                  pl.BlockSpec((RET_HEADS, ts, LANES), const3),
                  pl.BlockSpec((RET_HEADS, ts, LANES), const3),
                  pl.BlockSpec((None, RET_V + CONV_WIDTH, d), lambda t: (layer, 0, 0),
                               pipeline_mode=pl.Buffered(1)),
                  pl.BlockSpec((ts, d), lambda t: (done(t), 0)),
                  pl.BlockSpec((1, d), lambda t: (0, 0))],
        out_specs=[pl.BlockSpec((ts, d), lambda t: (done(t), 0)),
                   pl.BlockSpec((ts, d), lambda t: (done(t), 0))],
        out_shape=[jax.ShapeDtypeStruct((s, d), F32), jax.ShapeDtypeStruct((s, d), BF16)],
        scratch_shapes=[pltpu.VMEM((RET_HEADS, LANES, RET_DV), F32),
                        pltpu.VMEM((SUBLANES + ts, CONV_WIDTH), F32),
                        pltpu.VMEM((RET_V + CONV_WIDTH, d), BF16),
                        pltpu.VMEM((ts, RET_V + CONV_WIDTH), BF16),
                        pltpu.VMEM((ts, RET_V + CONV_WIDTH), BF16)],
        compiler_params=_compiler_params(("arbitrary",)),
        name="retention_conv_out_proj",
    )(qk, qk, rest, rest, rest, rest, rest,
      jnp.pad(conv_w, ((0, SUBLANES - CONV_K), (0, 0))), decay, q_dec, k_dec,
      w_out, x, gain.reshape(1, d))


def _lane_partial_sum(p):
    total = p[:, 0:LANES]
    for t in range(1, p.shape[1] // LANES):
        total = total + p[:, t * LANES:(t + 1) * LANES]
    return total


def _attn_kernel(bound_ref, q_ref, k_ref, v_ref, lq1_ref, lk1_ref, lq2_ref, lk2_ref,
                 subln_ref, o_ref, m_ref, l_ref, acc_ref, *, lambda_init):
    tq = q_ref.shape[0]
    tk = ATT_TK
    d = DIFF_HEAD_DIM
    qi = pl.program_id(1)
    bound = bound_ref[0]

    def scores(j, i, rows, masked):
        start = pl.multiple_of(j * tk, tk)
        s = lax.dot_general(q_ref[rows, i * d:(i + 1) * d],
                            k_ref[pl.ds(start, tk), i * d:(i + 1) * d],
                            (((1,), (1,)), ((), ())), preferred_element_type=F32)
        if masked:
            row = lax.broadcasted_iota(jnp.int32, s.shape, 0)
            col = lax.broadcasted_iota(jnp.int32, s.shape, 1)
            s = jnp.where(col <= row, s, MASK_VALUE)
        return s, v_ref[pl.ds(start, tk), :]

    def update_bounded(j, rows, masked, first=False):
        for i in range(2):
            s, v = scores(j, i, rows, masked)
            p = jnp.exp2(s - bound)
            pv = jnp.dot(p.astype(BF16), v, preferred_element_type=F32)
            if first:
                l_ref[i, rows, :] = _lane_partial_sum(p)
                acc_ref[i, rows, :] = pv
            else:
                l_ref[i, rows, :] += _lane_partial_sum(p)
                acc_ref[i, rows, :] += pv

    def update_online(j, rows, masked, first=False):
        for i in range(2):
            s, v = scores(j, i, rows, masked)
            m_cur = jnp.max(s, axis=1, keepdims=True)
            if first:
                m_new = jnp.broadcast_to(m_cur, (s.shape[0], LANES))
                p = jnp.exp2(s - m_cur)
                l_ref[i, rows, :] = _lane_partial_sum(p)
                acc_ref[i, rows, :] = jnp.dot(p.astype(BF16), v, preferred_element_type=F32)
            else:
                m_prev = m_ref[i, rows, :]
                m_new = jnp.maximum(m_prev, m_cur)
                alpha = jnp.exp2(m_prev - m_new)
                p = jnp.exp2(s - m_new[:, :1])
                l_ref[i, rows, :] = alpha * l_ref[i, rows, :] + _lane_partial_sum(p)
                acc_ref[i, rows, :] = alpha[:, :1] * acc_ref[i, rows, :] + jnp.dot(
                    p.astype(BF16), v, preferred_element_type=F32)
            m_ref[i, rows, :] = m_new

    def run(update, unroll):
        blocks_per_step = tq // tk
        all_rows = slice(0, tq)

        for r in range(blocks_per_step):
            update(blocks_per_step * qi + r, slice(r * tk, tq), masked=True, first=(r == 0))

        def body(t, carry):
            for u in range(unroll):
                update(unroll * t + u, all_rows, masked=False)
            return carry
        lax.fori_loop(0, (blocks_per_step // unroll) * qi, body, 0)

    @pl.when(bound <= ATT_BOUND_MAX)
    def _():
        run(update_bounded, ATT_UNROLL)

    @pl.when(bound > ATT_BOUND_MAX)
    def _():
        run(update_online, 1)

    lam = (jnp.exp(jnp.sum(lq1_ref[...] * lk1_ref[...], axis=1, keepdims=True))
           - jnp.exp(jnp.sum(lq2_ref[...] * lk2_ref[...], axis=1, keepdims=True))
           + lambda_init)
    l0 = jnp.sum(l_ref[0], axis=1, keepdims=True)
    l1 = jnp.sum(l_ref[1], axis=1, keepdims=True)
    o = acc_ref[0] / l0 - lam * (acc_ref[1] / l1)
    o = _rms_rows(o) * subln_ref[...] * (1.0 - lambda_init)
    o_ref[...] = o.astype(o_ref.dtype)


def _diff_attention(qk, v, score_bound, lq1, lk1, lq2, lk2, subln, lambda_init):
    s = qk.shape[0]
    hd = 2 * DIFF_HEAD_DIM
    assert ATT_TQ % ATT_TK == 0
    vec = lambda a: a.reshape(1, -1).astype(F32)
    small = lambda w: pl.BlockSpec((1, w), lambda h, i: (0, 0))
    kernel = functools.partial(_attn_kernel, lambda_init=lambda_init)
    return pl.pallas_call(
        kernel,
        grid=(DIFF_HEADS, s // ATT_TQ),
        in_specs=[pl.BlockSpec(memory_space=pltpu.SMEM),
                  pl.BlockSpec((ATT_TQ, hd), lambda h, i: (i, h)),
                  pl.BlockSpec((s, hd), lambda h, i: (0, DIFF_HEADS + h)),
                  pl.BlockSpec((s, hd), lambda h, i: (0, h)),
                  small(DIFF_HEAD_DIM), small(DIFF_HEAD_DIM),
                  small(DIFF_HEAD_DIM), small(DIFF_HEAD_DIM), small(hd)],
        out_specs=pl.BlockSpec((ATT_TQ, hd), lambda h, i: (i, h)),
        out_shape=jax.ShapeDtypeStruct((s, DIFF_HEADS * hd), BF16),
        scratch_shapes=[pltpu.VMEM((2, ATT_TQ, LANES), F32),
                        pltpu.VMEM((2, ATT_TQ, LANES), F32),
                        pltpu.VMEM((2, ATT_TQ, hd), F32)],
        compiler_params=_compiler_params(("parallel", "arbitrary")),
        name="diff_attention",
    )(score_bound, qk, qk, v, vec(lq1), vec(lk1), vec(lq2), vec(lk2), vec(subln))


def _out_kernel(a_ref, w_ref, x_ref, g_ref, xo_ref, xn_ref, wb_ref):
    @pl.when(pl.program_id(0) == 0)
    def _():
        wb_ref[...] = w_ref[...].astype(BF16)

    x = x_ref[...] + jnp.dot(a_ref[...], wb_ref[...], preferred_element_type=F32)
    xo_ref[...] = x
    xn_ref[...] = (_rms_rows(x) * g_ref[...]).astype(xn_ref.dtype)


def _out_project(a, w, layer, x, g):
    s, d = x.shape
    k = a.shape[1]
    row = lambda i: (i, 0)
    return pl.pallas_call(
        _out_kernel,
        grid=(s // OUT_TM,),
        in_specs=[pl.BlockSpec((OUT_TM, k), row),
                  pl.BlockSpec((None, k, d), lambda i: (layer, 0, 0),
                               pipeline_mode=pl.Buffered(1)),
                  pl.BlockSpec((OUT_TM, d), row),
                  pl.BlockSpec((1, d), lambda i: (0, 0))],
        out_specs=[pl.BlockSpec((OUT_TM, d), row), pl.BlockSpec((OUT_TM, d), row)],
        out_shape=[jax.ShapeDtypeStruct((s, d), F32), jax.ShapeDtypeStruct((s, d), BF16)],
        scratch_shapes=[pltpu.VMEM((k, d), BF16)],
        compiler_params=_compiler_params(("arbitrary",)),
        name="out_proj_residual_norm",
    )(a, w, x, g.reshape(1, d))


def _ffn_kernel(xn_ref, wg_ref, wu_ref, wd_ref, x_ref, *rest):
    g_ref, xo_ref, xn_out_ref = rest if len(rest) == 3 else (None, rest[0], None)
    f = pl.program_id(1)

    @pl.when(f == 0)
    def _():
        xo_ref[...] = x_ref[...]

    xn = xn_ref[...]
    gate = jnp.dot(xn, wg_ref[...].astype(BF16), preferred_element_type=F32)
    up = jnp.dot(xn, wu_ref[...].astype(BF16), preferred_element_type=F32)
    act = (_silu(gate) * up).astype(BF16)
    xo_ref[...] += jnp.dot(act, wd_ref[...].astype(BF16), preferred_element_type=F32)

    if xn_out_ref is not None:
        @pl.when(f == pl.num_programs(1) - 1)
        def _():
            xn_out_ref[...] = (_rms_rows(xo_ref[...]) * g_ref[...]).astype(xn_out_ref.dtype)


def _ffn(xn, wg, wu, wd, layer, x, g):
    s, d = x.shape
    hidden = wg.shape[2]
    row_block = lambda **kw: pl.BlockSpec((FFN_TM, d), lambda i, f: (i, 0), **kw)
    in_specs = [row_block(),
                pl.BlockSpec((None, d, FFN_TF), lambda i, f: (layer, 0, f)),
                pl.BlockSpec((None, d, FFN_TF), lambda i, f: (layer, 0, f)),
                pl.BlockSpec((None, FFN_TF, d), lambda i, f: (layer, f, 0)),
                row_block()]
    out_specs = [row_block()]
    out_shape = [jax.ShapeDtypeStruct((s, d), F32)]
    args = [xn, wg, wu, wd, x]
    if g is not None:
        in_specs.append(pl.BlockSpec((1, d), lambda i, f: (0, 0)))
        out_specs.append(row_block())
        out_shape.append(jax.ShapeDtypeStruct((s, d), BF16))
        args.append(g.reshape(1, d))
    outs = pl.pallas_call(
        _ffn_kernel,
        grid=(s // FFN_TM, hidden // FFN_TF),
        in_specs=in_specs,
        out_specs=out_specs,
        out_shape=out_shape,
        compiler_params=_compiler_params(("parallel", "arbitrary")),
        name="swiglu_ffn_residual_norm" if g is not None else "swiglu_ffn_residual",
    )(*args)
    return (outs[0], outs[1]) if g is not None else (outs[0], None)


def _rope_tables(seq, dim):
    inv = np.float32(ROPE_THETA) ** (-np.arange(0, dim, 2, dtype=np.float32) / np.float32(dim))
    ang = np.arange(seq, dtype=np.float32)[:, None] * inv[None, :]
    cos = np.cos(ang.astype(np.float64)).astype(np.float32)
    sin = np.sin(ang.astype(np.float64)).astype(np.float32)
    reps = LANES // dim
    cos_l = np.tile(np.concatenate([cos, cos], axis=1), (1, reps))
    sin_l = np.tile(np.concatenate([-sin, sin], axis=1), (1, reps))
    return jnp.asarray(cos_l), jnp.asarray(sin_l)


def kernel(x, norm_mix, norm_ffn, hyb_w_in, hyb_conv_w, hyb_w_out, diff_w_qkv, diff_q_norm,
           diff_k_norm, diff_lambda_q1, diff_lambda_k1, diff_lambda_q2, diff_lambda_k2,
           diff_subln, diff_w_out, ffn_w_gate, ffn_w_up, ffn_w_down):
    b, s, d = x.shape
    assert b == 1 and d == D_MODEL
    xs = x.reshape(s, d)
    cos_r, sin_r = _rope_tables(s, RET_DK)
    cos_a, sin_a = _rope_tables(s, DIFF_HEAD_DIM)

    xn = _rmsnorm(xs, norm_mix[0])
    for layer in range(DEPTH):
        j = layer // 2
        if layer % 2 == 0:
            qk_cols = 2 * RET_QK
            qk = _project_rope(xn, hyb_w_in, j, qk_cols, cos_r, sin_r, RET_DK // 2, None)
            rest = _project(xn, hyb_w_in, j, qk_cols, HYB_IN - qk_cols)
            xs, xn = _hybrid_mix(qk, rest, hyb_conv_w[j], hyb_w_out, j, xs, norm_ffn[layer])
        else:
            lambda_init = 0.8 - 0.6 * math.exp(-0.3 * layer)
            n_groups = DIFF_QK_COLS // DIFF_HEAD_DIM
            q_gain = diff_q_norm[j] * (LOG2_E * DIFF_HEAD_DIM ** -0.5)
            gain = jnp.concatenate([jnp.tile(q_gain, n_groups),
                                    jnp.tile(diff_k_norm[j], n_groups)]).reshape(1, -1)
            qk_cols = 2 * DIFF_QK_COLS
            qk = _project_rope(xn, diff_w_qkv, j, qk_cols, cos_a, sin_a,
                               DIFF_HEAD_DIM // 2, gain)
            v = _project(xn, diff_w_qkv, j, qk_cols, DIFF_QKV - qk_cols)
            score_bound = (ATT_BOUND_SLACK * DIFF_HEAD_DIM * jnp.max(jnp.abs(q_gain))
                           * jnp.max(jnp.abs(diff_k_norm[j]))).reshape(1).astype(F32)
            mixed = _diff_attention(qk, v, score_bound, diff_lambda_q1[j], diff_lambda_k1[j],
                                    diff_lambda_q2[j], diff_lambda_k2[j], diff_subln[j],
                                    lambda_init)
            xs, xn = _out_project(mixed, diff_w_out, j, xs, norm_ffn[layer])
        g_next = norm_mix[layer + 1] if layer + 1 < DEPTH else None
        xs, xn = _ffn(xn, ffn_w_gate, ffn_w_up, ffn_w_down, layer, xs, g_next)
    return xs.reshape(b, s, d)
```
